```python
import math
import jax, jax.numpy as jnp
from jax import lax
import numpy as np

D_MODEL = 1024
BATCH = 4
SEQ = 4096
DEPTH = 4

HEAD_DIM = 64
NORM_EPS = 1e-6
GMLP_GROUPS = 8
GMLP_WIDTH = GMLP_GROUPS * HEAD_DIM
GMLP_CHUNK = 128
FOX_HEADS = 8
FOX_WIDTH = FOX_HEADS * HEAD_DIM
FOX_Q_BLOCK = 128
MOBA_HEADS = 8
MOBA_WIDTH = MOBA_HEADS * HEAD_DIM
MOBA_BLOCK = 256
MOBA_TOPK = 3
MOBA_Q_CHUNK = 16
ROPE_THETA = 500000.0
ROPE_DIM = HEAD_DIM // 4
SSM_HEADS = 12
SSM_HEAD_DIM = 64
SSM_WIDTH = SSM_HEADS * SSM_HEAD_DIM
SSM_GROUPS = 2
SSM_STATE = 128
SSM_CONV = 4
SSM_CHUNK = 128
SSM_CONV_DIM = SSM_WIDTH + 2 * SSM_GROUPS * SSM_STATE
N_BRANCH = 4
D_FF = 4 * D_MODEL
IN_SPLITS = (2 * GMLP_WIDTH, 3 * FOX_WIDTH, FOX_HEADS, 3 * MOBA_WIDTH,
             SSM_WIDTH, SSM_CONV_DIM, SSM_HEADS, N_BRANCH * D_MODEL)
IN_COLS = sum(IN_SPLITS)

kernel_name = 'conditioned_hybrid_gmlp_fox_moba_ssd_trunk'


def split_cols(t, sizes):
    outs, start = [], 0
    for s in sizes:
        outs.append(t[..., start:start + s])
        start += s
    return outs


def rms_norm(x, w):
    xf = x.astype(jnp.float32)
    xn = xf * lax.rsqrt(jnp.mean(xf * xf, axis=-1, keepdims=True) + NORM_EPS)
    return xn.astype(x.dtype) * w


def layer_norm(x, w, b):
    xf = x.astype(jnp.float32)
    xc = xf - jnp.mean(xf, axis=-1, keepdims=True)
    xn = xc * lax.rsqrt(jnp.mean(xc * xc, axis=-1, keepdims=True) + NORM_EPS)
    return xn.astype(x.dtype) * w + b


def partial_rotary(x):
    S = x.shape[1]
    half = ROPE_DIM // 2
    inv_freq = ROPE_THETA ** (-jnp.arange(half, dtype=jnp.float32) / half)
    ang = jnp.arange(S, dtype=jnp.float32)[:, None] * inv_freq[None, :]
    cos = jnp.cos(ang)[None, :, None, :]
    sin = jnp.sin(ang)[None, :, None, :]
    xr = x[..., :ROPE_DIM].astype(jnp.float32)
    x1, x2 = xr[..., :half], xr[..., half:]
    rot = jnp.concatenate([x1 * cos - x2 * sin, x2 * cos + x1 * sin], axis=-1).astype(x.dtype)
    return jnp.concatenate([rot, x[..., ROPE_DIM:]], axis=-1)


def chunked_gmlp(u, v, ln_w, ln_b, w_s, b_s):
    Bn, S, _ = v.shape
    vn = layer_norm(v, ln_w, ln_b).reshape(Bn, S // GMLP_CHUNK, GMLP_CHUNK, GMLP_GROUPS, HEAD_DIM)
    causal = jnp.tril(jnp.ones((GMLP_CHUNK, GMLP_CHUNK), dtype=bool))
    ws = jnp.where(causal[None], w_s, 0.0)
    mixed = jnp.einsum('gts,bnsgc->bntgc', ws, vn) + b_s.T[None, None, :, :, None]
    return u * mixed.reshape(Bn, S, GMLP_WIDTH)


def forgetting_attention(q, k, v, f_logit, f_bias):
    Bn, S, H, Dh = q.shape
    log_f = jax.nn.log_sigmoid((f_logit + f_bias).astype(jnp.float32))
    cum = jnp.cumsum(log_f, axis=1).transpose(0, 2, 1)
    qh, kh, vh = (t.transpose(0, 2, 1, 3) for t in (q, k, v))
    scale = Dh ** -0.5
    key_pos = jnp.arange(S)

    def block(i):
        start = i * FOX_Q_BLOCK
        qb = lax.dynamic_slice_in_dim(qh, start, FOX_Q_BLOCK, axis=2)
        cq = lax.dynamic_slice_in_dim(cum, start, FOX_Q_BLOCK, axis=2)
        s = jnp.einsum('bhqd,bhkd->bhqk', qb, kh, preferred_element_type=jnp.float32) * scale
        s = s + (cq[..., :, None] - cum[..., None, :])
        q_pos = start + jnp.arange(FOX_Q_BLOCK)
        s = jnp.where(q_pos[:, None] >= key_pos[None, :], s, -jnp.inf)
        p = jax.nn.softmax(s, axis=-1).astype(v.dtype)
        return jnp.einsum('bhqk,bhkd->bhqd', p, vh)

    out = lax.map(block, jnp.arange(S // FOX_Q_BLOCK))
    return out.transpose(1, 0, 3, 2, 4).reshape(Bn, S, H * Dh)


def moba_attention(q, k, v):
    Bn, S, H, Dh = q.shape
    nb = -(-S // MOBA_BLOCK)
    pad = nb * MOBA_BLOCK - S
    topk = min(MOBA_TOPK, nb)
    qh = q.transpose(0, 2, 1, 3)
    padw = ((0, 0), (0, pad), (0, 0), (0, 0))
    kp = jnp.pad(k, padw).transpose(0, 2, 1, 3).reshape(Bn, H, nb, MOBA_BLOCK, Dh)
    vp = jnp.pad(v, padw).transpose(0, 2, 1, 3).reshape(Bn, H, nb, MOBA_BLOCK, Dh)
    k_mean = jnp.mean(kp.astype(jnp.float32), axis=3)
    scale = Dh ** -0.5
    blk = jnp.arange(nb)
    in_blk = jnp.arange(MOBA_BLOCK)
    b_idx = jnp.arange(Bn)[:, None, None, None]
    h_idx = jnp.arange(H)[None, :, None, None]

    def chunk(i):
        start = i * MOBA_Q_CHUNK
        own = start // MOBA_BLOCK
        qc = lax.dynamic_slice_in_dim(qh, start, MOBA_Q_CHUNK, axis=2)
        q_pos = start + jnp.arange(MOBA_Q_CHUNK)
        gate = jnp.einsum('bhqd,bhnd->bhqn', qc.astype(jnp.float32), k_mean)
        gate = jnp.where(blk < own, gate, -jnp.inf)
        _, sel = lax.top_k(gate, topk)
        valid = sel < own
        k_sel = kp[b_idx, h_idx, sel]
        v_sel = vp[b_idx, h_idx, sel]
        s_sel = jnp.einsum('bhqd,bhqntd->bhqnt', qc, k_sel, preferred_element_type=jnp.float32) * scale
        s_sel = jnp.where(valid[..., None], s_sel, -jnp.inf).reshape(Bn, H, MOBA_Q_CHUNK, topk * MOBA_BLOCK)
        k_own = lax.dynamic_index_in_dim(kp, own, axis=2, keepdims=False)
        v_own = lax.dynamic_index_in_dim(vp, own, axis=2, keepdims=False)
        s_own = jnp.einsum('bhqd,bhtd->bhqt', qc, k_own, preferred_element_type=jnp.float32) * scale
        key_pos = own * MOBA_BLOCK + in_blk
        s_own = jnp.where(key_pos[None, :] <= q_pos[:, None], s_own, -jnp.inf)
        p = jax.nn.softmax(jnp.concatenate([s_sel, s_own], axis=-1), axis=-1).astype(v.dtype)
        p_sel = p[..., :topk * MOBA_BLOCK].reshape(Bn, H, MOBA_Q_CHUNK, topk, MOBA_BLOCK)
        p_own = p[..., topk * MOBA_BLOCK:]
        return (jnp.einsum('bhqnt,bhqntd->bhqd', p_sel, v_sel)
                + jnp.einsum('bhqt,bhtd->bhqd', p_own, v_own))

    out = lax.map(chunk, jnp.arange(S // MOBA_Q_CHUNK))
    return out.transpose(1, 0, 3, 2, 4).reshape(Bn, S, H * Dh)


def causal_depthwise_conv(x, w, b):
    K, C = w.shape
    y = lax.conv_general_dilated(x, w[:, None, :].astype(x.dtype), window_strides=(1,),
                                 padding=[(K - 1, 0)], dimension_numbers=('NWC', 'WIO', 'NWC'),
                                 feature_group_count=C)
    return y + b


def segsum(a):
    T = a.shape[-1]
    cs = jnp.cumsum(a, axis=-1)
    d = cs[..., :, None] - cs[..., None, :]
    return jnp.where(jnp.tril(jnp.ones((T, T), dtype=bool)), d, -jnp.inf)


def ssd_chunked_scan(x, dt, a, bm, cm):
    Bn, S, H, P = x.shape
    G, N = bm.shape[-2:]
    R = H // G
    nc, L = S // SSM_CHUNK, SSM_CHUNK
    xdt = (x.astype(jnp.float32) * dt[..., None]).reshape(Bn, nc, L, G, R, P)
    adt = (dt * a).reshape(Bn, nc, L, G, R).transpose(0, 3, 4, 1, 2)
    bc = bm.astype(jnp.float32).reshape(Bn, nc, L, G, N)
    cc = cm.astype(jnp.float32).reshape(Bn, nc, L, G, N)
    a_cum = jnp.cumsum(adt, axis=-1)
    decay_in = jnp.exp(segsum(adt))
    y_diag = jnp.einsum('bclgn,bcsgn,bgrcls,bcsgrp->bclgrp', cc, bc, decay_in, xdt)
    decay_to_end = jnp.exp(a_cum[..., -1:] - a_cum)
    states = jnp.einsum('bcsgn,bgrcs,bcsgrp->bcgrpn', bc, decay_to_end, xdt)
    chunk_decay = jnp.exp(a_cum[..., -1])

    def step(h, inp):
        st, dec = inp
        return h * dec[..., None, None] + st, h

    h0 = jnp.zeros((Bn, G, R, P, N), jnp.float32)
    _, prev = lax.scan(step, h0, (states.transpose(1, 0, 2, 3, 4, 5), chunk_decay.transpose(3, 0, 1, 2)))
    y_off = jnp.einsum('bclgn,cbgrpn,bgrcl->bclgrp', cc, prev, jnp.exp(a_cum))
    return (y_diag + y_off).reshape(Bn, S, H, P)


def mamba2_mixer(z, xbc, dt_raw, conv_w, conv_b, dt_bias, a_log, d_skip, norm_w):
    Bn, S, _ = z.shape
    xbc = jax.nn.silu(causal_depthwise_conv(xbc, conv_w, conv_b))
    xs, bm, cm = split_cols(xbc, (SSM_WIDTH, SSM_GROUPS * SSM_STATE, SSM_GROUPS * SSM_STATE))
    xs = xs.reshape(Bn, S, SSM_HEADS, SSM_HEAD_DIM)
    bm = bm.reshape(Bn, S, SSM_GROUPS, SSM_STATE)
    cm = cm.reshape(Bn, S, SSM_GROUPS, SSM_STATE)
    dt = jax.nn.softplus((dt_raw + dt_bias).astype(jnp.float32))
    a = -jnp.exp(a_log.astype(jnp.float32))
    y = ssd_chunked_scan(xs, dt, a, bm, cm) + d_skip.astype(jnp.float32)[:, None] * xs.astype(jnp.float32)
    y = y.reshape(Bn, S, SSM_WIDTH).astype(z.dtype) * jax.nn.silu(z)
    yg = rms_norm(y.reshape(Bn, S, SSM_GROUPS, SSM_WIDTH // SSM_GROUPS),
                  norm_w.reshape(SSM_GROUPS, SSM_WIDTH // SSM_GROUPS))
    return yg.reshape(Bn, S, SSM_WIDTH)


def hybrid_mixer(h, w_in, gmlp_ln_w, gmlp_ln_b, gmlp_ws, gmlp_bs, fox_f_bias,
                 ssm_conv_w, ssm_conv_b, ssm_dt_bias, ssm_a_log, ssm_d, ssm_norm_w,
                 w_branch_a, w_branch_b, w_branch_c, w_branch_d, w_out):
    Bn, S, _ = h.shape
    proj = h @ w_in
    uv, fox_qkv, fox_f, moba_qkv, ssm_z, ssm_xbc, ssm_dt, gate_logits = split_cols(proj, IN_SPLITS)
    u, v = jnp.split(jax.nn.gelu(uv), 2, axis=-1)
    y_a = chunked_gmlp(u, v, gmlp_ln_w, gmlp_ln_b, gmlp_ws, gmlp_bs)
    qf, kf, vf = [t.reshape(Bn, S, FOX_HEADS, HEAD_DIM) for t in jnp.split(fox_qkv, 3, axis=-1)]
    y_b = forgetting_attention(qf, kf, vf, fox_f, fox_f_bias)
    qm, km, vm = [t.reshape(Bn, S, MOBA_HEADS, HEAD_DIM) for t in jnp.split(moba_qkv, 3, axis=-1)]
    y_c = moba_attention(partial_rotary(qm), partial_rotary(km), vm)
    y_d = mamba2_mixer(ssm_z, ssm_xbc, ssm_dt, ssm_conv_w, ssm_conv_b, ssm_dt_bias,
                       ssm_a_log, ssm_d, ssm_norm_w)
    g = jax.nn.sigmoid(gate_logits).reshape(Bn, S, N_BRANCH, D_MODEL)
    merged = (g[:, :, 0] * (y_a @ w_branch_a) + g[:, :, 1] * (y_b @ w_branch_b)
              + g[:, :, 2] * (y_c @ w_branch_c) + g[:, :, 3] * (y_d @ w_branch_d))
    return merged @ w_out


def setup_inputs(seed: int = 0) -> dict:
    key = jax.random.key(seed)
    ks = jax.random.split(key, 28)
    L, D = DEPTH, D_MODEL
    f32 = jnp.float32

    def nrm(k, shape, scale):
        return jax.random.normal(k, shape, f32) * scale

    dt_init = jnp.exp(jax.random.uniform(ks[12], (L, SSM_HEADS), f32, math.log(1e-3), math.log(1e-1)))
    return {
        'x': nrm(ks[0], (BATCH, SEQ, D), 1.0),
        'c': nrm(ks[1], (BATCH, D), 1.0),
        'ada_w': nrm(ks[2], (L, D, 6 * D), 0.5 * D ** -0.5),
        'ada_b': nrm(ks[3], (L, 6 * D), 0.02),
        'norm_mix_w': 1.0 + nrm(ks[4], (L, D), 0.02),
        'w_in': nrm(ks[5], (L, D, IN_COLS), D ** -0.5),
        'gmlp_ln_w': 1.0 + nrm(ks[6], (L, GMLP_WIDTH), 0.02),
        'gmlp_ln_b': nrm(ks[7], (L, GMLP_WIDTH), 0.02),
        'gmlp_ws': nrm(ks[8], (L, GMLP_GROUPS, GMLP_CHUNK, GMLP_CHUNK), GMLP_CHUNK ** -0.5),
        'gmlp_bs': 1.0 + nrm(ks[9], (L, GMLP_GROUPS, GMLP_CHUNK), 0.1),
        'fox_f_bias': jax.random.uniform(ks[10], (L, FOX_HEADS), f32, 1.0, 6.0),
        'ssm_conv_w': nrm(ks[11], (L, SSM_CONV, SSM_CONV_DIM), SSM_CONV ** -0.5),
        'ssm_conv_b': nrm(ks[13], (L, SSM_CONV_DIM), 0.02),
        'ssm_dt_bias': dt_init + jnp.log(-jnp.expm1(-dt_init)),
        'ssm_a_log': jnp.log(jax.random.uniform(ks[14], (L, SSM_HEADS), f32, 1.0, 16.0)),
        'ssm_d': 1.0 + nrm(ks[15], (L, SSM_HEADS), 0.1),
        'ssm_norm_w': 1.0 + nrm(ks[16], (L, SSM_WIDTH), 0.02),
        'w_branch_a': nrm(ks[17], (L, GMLP_WIDTH, D), GMLP_WIDTH ** -0.5),
        'w_branch_b': nrm(ks[18], (L, FOX_WIDTH, D), FOX_WIDTH ** -0.5),
        'w_branch_c': nrm(ks[19], (L, MOBA_WIDTH, D), MOBA_WIDTH ** -0.5),
        'w_branch_d': nrm(ks[20], (L, SSM_WIDTH, D), SSM_WIDTH ** -0.5),
        'w_out': nrm(ks[21], (L, D, D), D ** -0.5),
        'norm_mlp_w': 1.0 + nrm(ks[22], (L, D), 0.02),
        'mlp_w1': nrm(ks[23], (L, D, D_FF), D ** -0.5),
        'mlp_w2': nrm(ks[24], (L, D_FF, D), D_FF ** -0.5),
        'final_norm_w': 1.0 + nrm(ks[25], (D,), 0.02),
    }


def reference(x, c, ada_w, ada_b, norm_mix_w, w_in, gmlp_ln_w, gmlp_ln_b, gmlp_ws, gmlp_bs,
              fox_f_bias, ssm_conv_w, ssm_conv_b, ssm_dt_bias, ssm_a_log, ssm_d, ssm_norm_w,
              w_branch_a, w_branch_b, w_branch_c, w_branch_d, w_out, norm_mlp_w, mlp_w1, mlp_w2,
              final_norm_w):
    c_act = jax.nn.silu(c)
    for l in range(DEPTH):
        mod = c_act @ ada_w[l] + ada_b[l]
        sh1, sc1, g1, sh2, sc2, g2 = [m[:, None, :] for m in jnp.split(mod, 6, axis=-1)]
        h = rms_norm(x, norm_mix_w[l]) * (1.0 + sc1) + sh1
        x = x + g1 * hybrid_mixer(h, w_in[l], gmlp_ln_w[l], gmlp_ln_b[l], gmlp_ws[l], gmlp_bs[l],
                                  fox_f_bias[l], ssm_conv_w[l], ssm_conv_b[l], ssm_dt_bias[l],
                                  ssm_a_log[l], ssm_d[l], ssm_norm_w[l], w_branch_a[l],
                                  w_branch_b[l], w_branch_c[l], w_branch_d[l], w_out[l])
        h = rms_norm(x, norm_mlp_w[l]) * (1.0 + sc2) + sh2
        x = x + g2 * (jnp.square(jax.nn.relu(h @ mlp_w1[l])) @ mlp_w2[l])
    return rms_norm(x, final_norm_w)
```

```python
import functools
import math

import jax
import jax.numpy as jnp
from jax import lax
from jax.experimental import pallas as pl
from jax.experimental.pallas import tpu as pltpu

F32 = jnp.float32
BF16 = jnp.bfloat16

HEAD_DIM = 64
NORM_EPS = 1e-6
GMLP_GROUPS = 8
GMLP_WIDTH = GMLP_GROUPS * HEAD_DIM
GMLP_CHUNK = 128
FOX_HEADS = 8
FOX_WIDTH = FOX_HEADS * HEAD_DIM
MOBA_HEADS = 8
MOBA_WIDTH = MOBA_HEADS * HEAD_DIM
MOBA_BLOCK = 256
MOBA_TOPK = 3
ROPE_THETA = 500000.0
ROPE_DIM = HEAD_DIM // 4
SSM_HEADS = 12
SSM_HEAD_DIM = 64
SSM_WIDTH = SSM_HEADS * SSM_HEAD_DIM
SSM_GROUPS = 2
SSM_STATE = 128
SSM_CONV = 4
SSM_CHUNK = 128
SSM_CONV_DIM = SSM_WIDTH + 2 * SSM_GROUPS * SSM_STATE
N_BRANCH = 4

LANES = 128
SUBLANES = 8
NEG_BIG = -1e30
VMEM_LIMIT = 48 * 1024 * 1024


def _cparams(sem):
    return pltpu.CompilerParams(dimension_semantics=sem, vmem_limit_bytes=VMEM_LIMIT)


def _sigmoid(x):
    return 1.0 / (1.0 + jnp.exp(-x))


def _softplus(x):
    return jnp.maximum(x, 0.0) + jnp.log(1.0 + jnp.exp(-jnp.abs(x)))


def _dot(a, b):
    return jnp.dot(a, b, preferred_element_type=F32)


def _dot_nt(a, b):
    return lax.dot_general(a, b, (((1,), (1,)), ((), ())), preferred_element_type=F32)


def _split3(x):
    hi = x.astype(BF16)
    r = x - hi.astype(F32)
    mid = r.astype(BF16)
    lo = (r - mid.astype(F32)).astype(BF16)
    return hi, mid, lo


def _tri_cumsum(x):
    t = x.shape[0]
    row = lax.broadcasted_iota(jnp.int32, (t, t), 0)
    col = lax.broadcasted_iota(jnp.int32, (t, t), 1)
    tri = jnp.where(row >= col, 1.0, 0.0).astype(BF16)
    hi, mid, lo = _split3(x)
    return _dot(tri, hi) + _dot(tri, mid) + _dot(tri, lo)


def _dot_f32(a, b):
    ah, am, al = _split3(a)
    bh, bm, bl = _split3(b)
    small = _dot(am, bm) + _dot(ah, bl) + _dot(al, bh)
    return _dot(ah, bh) + (_dot(ah, bm) + _dot(am, bh)) + small


def _mod_kernel(c_ref, w_ref, b_ref, o_ref):
    c = c_ref[...]
    ca = c * _sigmoid(c)
    o_ref[0] = _dot_f32(ca, w_ref[0]) + b_ref[0]


def _modulation(c, ada_w, ada_b):
    nb, d = c.shape
    nl = ada_w.shape[0]
    bp = -(-nb // SUBLANES) * SUBLANES
    cp = jnp.pad(c, ((0, bp - nb), (0, 0)))
    return pl.pallas_call(
        _mod_kernel,
        grid=(nl, 6),
        in_specs=[pl.BlockSpec((bp, d), lambda l, k: (0, 0)),
                  pl.BlockSpec((1, d, d), lambda l, k: (l, 0, k)),
                  pl.BlockSpec((1, 1, d), lambda l, k: (l, 0, k))],
        out_specs=pl.BlockSpec((1, bp, d), lambda l, k: (l, 0, k)),
        out_shape=jax.ShapeDtypeStruct((nl, bp, 6 * d), F32),
        compiler_params=_cparams(("parallel", "parallel")),
        name="adaln_mod",
    )(cp, ada_w, ada_b.reshape(nl, 1, 6 * d))


def _rms_mod(x, nw, sc, sh):
    ms = jnp.mean(x * x, axis=-1, keepdims=True)
    xn = x * lax.rsqrt(ms + NORM_EPS)
    return xn * nw * (1.0 + sc) + sh


def _norm_kernel(x_ref, nw_ref, sc_ref, sh_ref, o_ref):
    o_ref[...] = _rms_mod(x_ref[...], nw_ref[...], sc_ref[0], sh_ref[0]).astype(o_ref.dtype)


def _norm_mod(x, nw, mod, k_sc, k_sh, seq, tm):
    m, d = x.shape
    nb = m // seq
    return pl.pallas_call(
        _norm_kernel,
        grid=(m // tm,),
        in_specs=[pl.BlockSpec((tm, d), lambda i: (i, 0)),
                  pl.BlockSpec((1, d), lambda i: (0, 0)),
                  pl.BlockSpec((1, 1, d), lambda i: (k_sc * nb + (i * tm) // seq, 0, 0)),
                  pl.BlockSpec((1, 1, d), lambda i: (k_sh * nb + (i * tm) // seq, 0, 0))],
        out_specs=pl.BlockSpec((tm, d), lambda i: (i, 0)),
        out_shape=jax.ShapeDtypeStruct((m, d), BF16),
        compiler_params=_cparams(("parallel",)),
        name="norm_mod",
    )(x, nw.reshape(1, d), mod, mod)


def _final_norm_kernel(x_ref, nw_ref, o_ref):
    x = x_ref[...]
    ms = jnp.mean(x * x, axis=-1, keepdims=True)
    o_ref[...] = x * lax.rsqrt(ms + NORM_EPS) * nw_ref[...]


def _final_norm(x, nw, tm):
    m, d = x.shape
    return pl.pallas_call(
        _final_norm_kernel,
        grid=(m // tm,),
        in_specs=[pl.BlockSpec((tm, d), lambda i: (i, 0)),
                  pl.BlockSpec((1, d), lambda i: (0, 0))],
        out_specs=pl.BlockSpec((tm, d), lambda i: (i, 0)),
        out_shape=jax.ShapeDtypeStruct((m, d), F32),
        compiler_params=_cparams(("parallel",)),
        name="final_norm",
    )(x, nw.reshape(1, d))


def _mm_kernel(a_ref, w_ref, o_ref):
    o_ref[...] = _dot(a_ref[...], w_ref[...]).astype(o_ref.dtype)


def _matmul(a, w, out_dtype, tm, tn, name):
    m, k = a.shape
    n = w.shape[1]
    return pl.pallas_call(
        _mm_kernel,
        grid=(n // tn, m // tm),
        in_specs=[pl.BlockSpec((tm, k), lambda j, i: (i, 0)),
                  pl.BlockSpec((k, tn), lambda j, i: (0, j))],
        out_specs=pl.BlockSpec((tm, tn), lambda j, i: (i, j)),
        out_shape=jax.ShapeDtypeStruct((m, n), out_dtype),
        compiler_params=_cparams(("parallel", "parallel")),
        name=name,
    )(a, w)


def _gelu_tanh(x):
    c = math.sqrt(2.0 / math.pi)
    return 0.5 * x * (1.0 + jnp.tanh(c * (x + 0.044715 * (x * x * x))))


def _gmlp_kernel(uv_ref, lnw_ref, lnb_ref, ws_ref, bs_ref, o_ref, *, tm):
    g = _gelu_tanh(uv_ref[...])
    u = g[:, :GMLP_WIDTH]
    v = g[:, GMLP_WIDTH:]
    mu = jnp.mean(v, axis=-1, keepdims=True)
    vc = v - mu
    var = jnp.mean(vc * vc, axis=-1, keepdims=True)
    vn = (vc * lax.rsqrt(var + NORM_EPS) * lnw_ref[...] + lnb_ref[...]).astype(BF16)
    t = GMLP_CHUNK
    row = lax.broadcasted_iota(jnp.int32, (t, t), 0)
    col = lax.broadcasted_iota(jnp.int32, (t, t), 1)
    causal = row >= col
    ws = [jnp.where(causal, ws_ref[i], 0.0).astype(BF16) for i in range(GMLP_GROUPS)]
    first_head = lax.broadcasted_iota(jnp.int32, (t, LANES), 1) < HEAD_DIM
    for c in range(tm // t):
        rows = slice(c * t, (c + 1) * t)
        for p in range(GMLP_WIDTH // LANES):
            cols = slice(p * LANES, (p + 1) * LANES)
            vp = vn[rows, cols]
            mixed = jnp.where(first_head, _dot(ws[2 * p], vp), _dot(ws[2 * p + 1], vp))
            o_ref[rows, cols] = (u[rows, cols] * (mixed + bs_ref[:, cols])).astype(o_ref.dtype)


def _gmlp(uv, ln_w, ln_b, ws, bs, tm):
    m = uv.shape[0]
    w = GMLP_WIDTH
    bs_full = jnp.repeat(bs.T, HEAD_DIM, axis=1)
    return pl.pallas_call(
        functools.partial(_gmlp_kernel, tm=tm),
        grid=(m // tm,),
        in_specs=[pl.BlockSpec((tm, 2 * w), lambda i: (i, 0)),
                  pl.BlockSpec((1, w), lambda i: (0, 0)),
                  pl.BlockSpec((1, w), lambda i: (0, 0)),
                  pl.BlockSpec((GMLP_GROUPS, GMLP_CHUNK, GMLP_CHUNK), lambda i: (0, 0, 0)),
                  pl.BlockSpec((GMLP_CHUNK, w), lambda i: (0, 0))],
        out_specs=pl.BlockSpec((tm, w), lambda i: (i, 0)),
        out_shape=jax.ShapeDtypeStruct((m, w), BF16),
        compiler_params=_cparams(("parallel",)),
        name="gmlp",
    )(uv, ln_w.reshape(1, w), ln_b.reshape(1, w), ws, bs_full)


def _fcum_kernel(f_ref, b_ref, o_ref, *, seq):
    t = LANES

    def body(c, carry):
        r0 = pl.multiple_of(c * t, t)
        z = f_ref[pl.ds(r0, t), :] + b_ref[...]
        logf = jnp.minimum(z, 0.0) - jnp.log(1.0 + jnp.exp(-jnp.abs(z)))
        cs = _tri_cumsum(logf) + carry
        o_ref[pl.ds(r0, t), :] = cs
        return cs[t - 1:t, :]

    lax.fori_loop(0, seq // t, body, jnp.zeros((1, LANES), F32))


def _forget_cumsum(proj, col_block, f_bias, seq):
    m = proj.shape[0]
    fb = jnp.pad(f_bias, (0, LANES - FOX_HEADS)).reshape(1, LANES)
    return pl.pallas_call(
        functools.partial(_fcum_kernel, seq=seq),
        grid=(m // seq,),
        in_specs=[pl.BlockSpec((seq, LANES), lambda b: (b, col_block)),
                  pl.BlockSpec((1, LANES), lambda b: (0, 0))],
        out_specs=pl.BlockSpec((seq, LANES), lambda b: (b, 0)),
        out_shape=jax.ShapeDtypeStruct((m, LANES), F32),
        compiler_params=_cparams(("parallel",)),
        name="fox_cumsum",
    )(proj, fb)


def _online_softmax_step(s, vb, m_scr, l_scr, acc_scr):
    m_prev = m_scr[...]
    m_new = jnp.maximum(m_prev, jnp.max(s, axis=1, keepdims=True))
    alpha = jnp.exp(m_prev - m_new)
    p = jnp.exp(s - m_new)
    l_scr[...] = alpha * l_scr[...] + jnp.sum(p, axis=1, keepdims=True)
    acc_scr[...] = alpha * acc_scr[...] + _dot(p.astype(BF16), vb)
    m_scr[...] = m_new


def _fox_kernel(q_ref, k_ref, v_ref, ck_ref, o_ref, m_scr, l_scr, acc_scr, *, t):
    qi = pl.program_id(2)
    lane = lax.broadcasted_iota(jnp.int32, (t, LANES), 1)
    row = lax.broadcasted_iota(jnp.int32, (t, t), 0)
    col = lax.broadcasted_iota(jnp.int32, (t, t), 1)
    causal = row >= col
    q = q_ref[...] * (HEAD_DIM ** -0.5)
    diag0 = pl.multiple_of(qi * t, t)
    outs = []
    for h in range(2):
        head_lanes = (lane < HEAD_DIM) if h == 0 else (lane >= HEAD_DIM)
        qh = jnp.where(head_lanes, q, 0.0).astype(BF16)
        ck_diag = ck_ref[0, qi, pl.ds(h, 1), :]
        c0 = jnp.max(ck_diag, axis=1, keepdims=True)
        m_scr[...] = jnp.full(m_scr.shape, -jnp.inf, F32)
        l_scr[...] = jnp.zeros(l_scr.shape, F32)
        acc_scr[...] = jnp.zeros(acc_scr.shape, F32)

        def body(j, carry):
            r0 = pl.multiple_of(j * t, t)
            s = _dot_nt(qh, k_ref[pl.ds(r0, t), :]) + (c0 - ck_ref[0, j, pl.ds(h, 1), :])
            _online_softmax_step(s, v_ref[pl.ds(r0, t), :], m_scr, l_scr, acc_scr)
            return carry

        lax.fori_loop(0, qi, body, 0)
        s = _dot_nt(qh, k_ref[pl.ds(diag0, t), :]) + (c0 - ck_diag)
        s = jnp.where(causal, s, NEG_BIG)
        _online_softmax_step(s, v_ref[pl.ds(diag0, t), :], m_scr, l_scr, acc_scr)
        outs.append(acc_scr[...] / l_scr[...])
    o_ref[...] = jnp.where(lane < HEAD_DIM, outs[0], outs[1]).astype(o_ref.dtype)


def _fox_attention(qkv, q_col, k_col, v_col, cum_t, seq, t):
    m = qkv.shape[0]
    nb = m // seq
    pairs = FOX_HEADS // 2
    nq = seq // t
    return pl.pallas_call(
        functools.partial(_fox_kernel, t=t),
        grid=(nb, pairs, nq),
        in_specs=[pl.BlockSpec((t, LANES), lambda b, p, i: (b * nq + i, q_col + p)),
                  pl.BlockSpec((seq, LANES), lambda b, p, i: (b, k_col + p)),
                  pl.BlockSpec((seq, LANES), lambda b, p, i: (b, v_col + p)),
                  pl.BlockSpec((1, nq, 2, t), lambda b, p, i: (b * pairs + p, 0, 0, 0))],
        out_specs=pl.BlockSpec((t, LANES), lambda b, p, i: (b * nq + i, p)),
        out_shape=jax.ShapeDtypeStruct((m, FOX_WIDTH), BF16),
        scratch_shapes=[pltpu.VMEM((t, 1), F32), pltpu.VMEM((t, 1), F32), pltpu.VMEM((t, LANES), F32)],
        compiler_params=_cparams(("parallel", "parallel", "arbitrary")),
        name="fox_attn",
    )(qkv, qkv, qkv, cum_t)


def _rope_tables(seq):
    half = ROPE_DIM // 2
    inv_freq = ROPE_THETA ** (-jnp.arange(half, dtype=F32) / half)
    ang = jnp.arange(seq, dtype=F32)[:, None] * inv_freq[None, :]
    cos, sin = jnp.cos(ang), jnp.sin(ang)
    ones = jnp.ones((seq, HEAD_DIM - ROPE_DIM), F32)
    cos_h = jnp.concatenate([cos, cos, ones], axis=1)
    sin_h = jnp.concatenate([-sin, sin, 0.0 * ones], axis=1)
    return jnp.tile(cos_h, (1, 2)), jnp.tile(sin_h, (1, 2))


def _rotary(x, cos_t, sin_t):
    half = ROPE_DIM // 2
    lane = lax.broadcasted_iota(jnp.int32, (x.shape[0], LANES), 1)
    first_half = (lane & (HEAD_DIM - 1)) < half
    outs = []
    for j in range(x.shape[1] // LANES):
        xc = x[:, j * LANES:(j + 1) * LANES]
        up = pltpu.roll(xc, LANES - half, axis=1)
        down = pltpu.roll(xc, half, axis=1)
        outs.append(xc * cos_t + jnp.where(first_half, up, down) * sin_t)
    return jnp.concatenate(outs, axis=1)


def _moba_k_kernel(k_ref, cos_ref, sin_ref, ko_ref, km_ref):
    kr = _rotary(k_ref[...], cos_ref[...], sin_ref[...])
    ko_ref[...] = kr.astype(ko_ref.dtype)
    km_ref[0] = jnp.mean(kr, axis=0, keepdims=True)


def _moba_k_prep(proj, k_col, cos_t, sin_t, seq):
    m = proj.shape[0]
    t = MOBA_BLOCK
    nblk = seq // t
    return pl.pallas_call(
        _moba_k_kernel,
        grid=(m // t,),
        in_specs=[pl.BlockSpec((t, MOBA_WIDTH), lambda i: (i, k_col)),
                  pl.BlockSpec((t, LANES), lambda i: (i % nblk, 0)),
                  pl.BlockSpec((t, LANES), lambda i: (i % nblk, 0))],
        out_specs=[pl.BlockSpec((t, MOBA_WIDTH), lambda i: (i, 0)),
                   pl.BlockSpec((1, 1, MOBA_WIDTH), lambda i: (i, 0, 0))],
        out_shape=[jax.ShapeDtypeStruct((m, MOBA_WIDTH), BF16),
                   jax.ShapeDtypeStruct((m // t, 1, MOBA_WIDTH), F32)],
        compiler_params=_cparams(("parallel",)),
        name="moba_k_prep",
    )(proj, cos_t, sin_t)


def _moba_q_kernel(q_ref, cos_ref, sin_ref, km_ref, qo_ref, ns_ref, *, nblk):
    own = pl.program_id(0) % nblk
    qr = _rotary(q_ref[...], cos_ref[...], sin_ref[...])
    qo_ref[...] = (qr * (HEAD_DIM ** -0.5)).astype(qo_ref.dtype)
    gate = _dot_f32(qr, km_ref[0])
    slots = LANES // MOBA_HEADS
    lane = lax.broadcasted_iota(jnp.int32, gate.shape, 1)
    n = lane & (slots - 1)
    cnt = jnp.zeros(gate.shape, jnp.int32)
    for r in range(1, slots):
        wrapped = (n + r) >= slots
        other = jnp.where(wrapped, pltpu.roll(gate, slots - r, axis=1), pltpu.roll(gate, LANES - r, axis=1))
        other_n = jnp.where(wrapped, n + (r - slots), n + r)
        beats = (other > gate) | ((other == gate) & wrapped)
        cnt = cnt + jnp.where((other_n < own) & beats, 1, 0)
    selected = (n < own) & (cnt < MOBA_TOPK)
    ns_ref[...] = jnp.where(selected, 0.0, 1.0).astype(ns_ref.dtype)


def _moba_q_prep(proj, q_col, cos_t, sin_t, km_mat, seq):
    m = proj.shape[0]
    t = MOBA_BLOCK
    nblk = seq // t
    return pl.pallas_call(
        functools.partial(_moba_q_kernel, nblk=nblk),
        grid=(m // t,),
        in_specs=[pl.BlockSpec((t, MOBA_WIDTH), lambda i: (i, q_col)),
                  pl.BlockSpec((t, LANES), lambda i: (i % nblk, 0)),
                  pl.BlockSpec((t, LANES), lambda i: (i % nblk, 0)),
                  pl.BlockSpec((1, MOBA_WIDTH, LANES), lambda i: (i // nblk, 0, 0))],
        out_specs=[pl.BlockSpec((t, MOBA_WIDTH), lambda i: (i, 0)),
                   pl.BlockSpec((t, LANES), lambda i: (i, 0))],
        out_shape=[jax.ShapeDtypeStruct((m, MOBA_WIDTH), BF16),
                   jax.ShapeDtypeStruct((m, LANES), BF16)],
        compiler_params=_cparams(("parallel",)),
        name="moba_q_prep",
    )(proj, cos_t, sin_t, km_mat)


def _moba_kernel(q_ref, ns_ref, k_ref, v_ref, o_ref, m_scr, l_scr, acc_scr):
    t = MOBA_BLOCK
    p = pl.program_id(1)
    qi = pl.program_id(2)
    lane = lax.broadcasted_iota(jnp.int32, (t, LANES), 1)
    row = lax.broadcasted_iota(jnp.int32, (t, t), 0)
    col = lax.broadcasted_iota(jnp.int32, (t, t), 1)
    causal = row >= col
    q = q_ref[...]
    ns = ns_ref[...]
    own0 = pl.multiple_of(qi * t, t)
    slots = LANES // MOBA_HEADS
    outs = []
    for h in range(2):
        head_lanes = (lane < HEAD_DIM) if h == 0 else (lane >= HEAD_DIM)
        qh = jnp.where(head_lanes, q, jnp.zeros_like(q))
        qcat = jnp.concatenate([qh, ns], axis=1)
        m_scr[...] = jnp.full(m_scr.shape, -jnp.inf, F32)
        l_scr[...] = jnp.zeros(l_scr.shape, F32)
        acc_scr[...] = jnp.zeros(acc_scr.shape, F32)
        s = jnp.where(causal, _dot_nt(qh, k_ref[pl.ds(own0, t), :]), NEG_BIG)
        _online_softmax_step(s, v_ref[pl.ds(own0, t), :], m_scr, l_scr, acc_scr)

        def body(j, carry):
            r0 = pl.multiple_of(j * t, t)
            slot = (2 * p + h) * slots + j
            kmask = jnp.where(lane == slot, NEG_BIG, 0.0).astype(BF16)
            kcat = jnp.concatenate([k_ref[pl.ds(r0, t), :], kmask], axis=1)
            _online_softmax_step(_dot_nt(qcat, kcat), v_ref[pl.ds(r0, t), :], m_scr, l_scr, acc_scr)
            return carry

        lax.fori_loop(0, qi, body, 0)
        outs.append(acc_scr[...] / l_scr[...])
    o_ref[...] = jnp.where(lane < HEAD_DIM, outs[0], outs[1]).astype(o_ref.dtype)


def _moba_attention(q_rot, not_sel, k_rot, v_arr, v_col, seq):
    m = q_rot.shape[0]
    t = MOBA_BLOCK
    nb = m // seq
    nq = seq // t
    pairs = MOBA_HEADS // 2
    return pl.pallas_call(
        _moba_kernel,
        grid=(nb, pairs, nq),
        in_specs=[pl.BlockSpec((t, LANES), lambda b, p, i: (b * nq + i, p)),
                  pl.BlockSpec((t, LANES), lambda b, p, i: (b * nq + i, 0)),
                  pl.BlockSpec((seq, LANES), lambda b, p, i: (b, p)),
                  pl.BlockSpec((seq, LANES), lambda b, p, i: (b, v_col + p))],
        out_specs=pl.BlockSpec((t, LANES), lambda b, p, i: (b * nq + i, p)),
        out_shape=jax.ShapeDtypeStruct((m, MOBA_WIDTH), BF16),
        scratch_shapes=[pltpu.VMEM((t, 1), F32), pltpu.VMEM((t, 1), F32), pltpu.VMEM((t, LANES), F32)],
        compiler_params=_cparams(("parallel", "parallel", "arbitrary")),
        name="moba_attn",
    )(q_rot, not_sel, k_rot, v_arr)


def _pair_cols(x, h0, h1, shape):
    lane = lax.broadcasted_iota(jnp.int32, shape, 1)
    a = jnp.broadcast_to(x[:, h0:h0 + 1], shape)
    b = jnp.broadcast_to(x[:, h1:h1 + 1], shape)
    return jnp.where(lane < SSM_HEAD_DIM, a, b)


def _ssd_kernel(in_ref, cw_ref, cb_ref, dtb_ref, alog_ref, dsk_ref, nw_ref, o_ref, xpad_scr, st_scr):
    t = SSM_CHUNK
    c = pl.program_id(1)
    halo = SUBLANES

    @pl.when(c == 0)
    def _():
        xpad_scr[0:halo, :] = jnp.zeros((halo, SSM_CONV_DIM), F32)
        st_scr[...] = jnp.zeros(st_scr.shape, F32)

    xpad_scr[halo:halo + t, :] = in_ref[:, 0:SSM_CONV_DIM]
    conv = cb_ref[...] + cw_ref[0:1, :] * xpad_scr[pl.ds(halo - SSM_CONV + 1, t), :]
    for k in range(1, SSM_CONV):
        conv = conv + cw_ref[k:k + 1, :] * xpad_scr[pl.ds(halo - SSM_CONV + 1 + k, t), :]
    xpad_scr[0:halo, :] = xpad_scr[t:t + halo, :]
    xbc = conv * _sigmoid(conv)
    z = in_ref[:, SSM_CONV_DIM:SSM_CONV_DIM + SSM_WIDTH]
    dt = _softplus(in_ref[:, SSM_CONV_DIM + SSM_WIDTH:] + dtb_ref[...])
    a = -jnp.exp(alog_ref[...])
    acum = _tri_cumsum(dt * a)
    acum_t = acum.T
    tot = acum[t - 1:t, :]
    decay_end = jnp.exp(tot - acum)
    decay_in = jnp.exp(acum)
    exp_tot = jnp.exp(tot)
    row = lax.broadcasted_iota(jnp.int32, (t, t), 0)
    col = lax.broadcasted_iota(jnp.int32, (t, t), 1)
    causal = row >= col
    lane = lax.broadcasted_iota(jnp.int32, (t, LANES), 1)
    lane1 = lax.broadcasted_iota(jnp.int32, (1, LANES), 1)
    pair_shape = (t, LANES)
    rep = SSM_HEADS // SSM_GROUPS
    b_off = SSM_WIDTH
    c_off = SSM_WIDTH + SSM_GROUPS * SSM_STATE
    ys = []
    for g in range(SSM_GROUPS):
        bm = xbc[:, b_off + g * SSM_STATE:b_off + (g + 1) * SSM_STATE]
        cm = xbc[:, c_off + g * SSM_STATE:c_off + (g + 1) * SSM_STATE].astype(BF16)
        cb = _dot_nt(cm, bm.astype(BF16))
        bm_t = bm.T.astype(BF16)
        for pr in range(rep // 2):
            j = g * (rep // 2) + pr
            h0, h1 = 2 * j, 2 * j + 1
            xp = xbc[:, j * LANES:(j + 1) * LANES]
            xdt = xp * _pair_cols(dt, h0, h1, pair_shape)
            xdt_b = xdt.astype(BF16)
            yd = []
            for h in (h0, h1):
                diff = acum[:, h:h + 1] - acum_t[h:h + 1, :]
                w = cb * jnp.exp(jnp.where(causal, diff, NEG_BIG))
                yd.append(_dot(w.astype(BF16), xdt_b))
            y_diag = jnp.where(lane < SSM_HEAD_DIM, yd[0], yd[1])
            prev = st_scr[j]
            y_off = _dot(cm, prev.astype(BF16)) * _pair_cols(decay_in, h0, h1, pair_shape)
            xw = (xdt * _pair_cols(decay_end, h0, h1, pair_shape)).astype(BF16)
            scale = jnp.where(lane1 < SSM_HEAD_DIM, exp_tot[:, h0:h0 + 1], exp_tot[:, h1:h1 + 1])
            st_scr[j] = prev * scale + _dot(bm_t, xw)
            ys.append(y_diag + y_off + dsk_ref[:, j * LANES:(j + 1) * LANES] * xp)
    y = jnp.concatenate(ys, axis=1)
    y = y * (z * _sigmoid(z))
    gw = SSM_WIDTH // SSM_GROUPS
    outs = []
    for g in range(SSM_GROUPS):
        seg = y[:, g * gw:(g + 1) * gw]
        ms = jnp.mean(seg * seg, axis=-1, keepdims=True)
        outs.append(seg * lax.rsqrt(ms + NORM_EPS) * nw_ref[:, g * gw:(g + 1) * gw])
    o_ref[...] = jnp.concatenate(outs, axis=1).astype(o_ref.dtype)


def _ssd(proj, conv_w, conv_b, dt_bias, a_log, d_skip, norm_w, seq):
    m, wtot = proj.shape
    t = SSM_CHUNK
    nc = seq // t
    pad = LANES - SSM_HEADS
    dtb = jnp.pad(dt_bias, (0, pad)).reshape(1, LANES)
    alog = jnp.pad(a_log, (0, pad)).reshape(1, LANES)
    dsk = jnp.repeat(d_skip, SSM_HEAD_DIM).reshape(1, SSM_WIDTH)
    const = lambda b, c: (0, 0)
    return pl.pallas_call(
        _ssd_kernel,
        grid=(m // seq, nc),
        in_specs=[pl.BlockSpec((t, wtot), lambda b, c: (b * nc + c, 0)),
                  pl.BlockSpec((SSM_CONV, SSM_CONV_DIM), const),
                  pl.BlockSpec((1, SSM_CONV_DIM), const),
                  pl.BlockSpec((1, LANES), const),
                  pl.BlockSpec((1, LANES), const),
                  pl.BlockSpec((1, SSM_WIDTH), const),
                  pl.BlockSpec((1, SSM_WIDTH), const)],
        out_specs=pl.BlockSpec((t, SSM_WIDTH), lambda b, c: (b * nc + c, 0)),
        out_shape=jax.ShapeDtypeStruct((m, SSM_WIDTH), BF16),
        scratch_shapes=[pltpu.VMEM((t + SUBLANES, SSM_CONV_DIM), F32),
                        pltpu.VMEM((SSM_HEADS // 2, SSM_STATE, LANES), F32)],
        compiler_params=_cparams(("parallel", "arbitrary")),
        name="ssd",
    )(proj, conv_w, conv_b.reshape(1, SSM_CONV_DIM), dtb, alog, dsk, norm_w.reshape(1, SSM_WIDTH))


def _merge_kernel(ya_ref, yb_ref, yc_ref, yd_ref, g_ref, x_ref, g1_ref,
                  wa_ref, wb_ref, wc_ref, wd_ref, wo_ref, o_ref):
    d = x_ref.shape[1]
    merged = _sigmoid(g_ref[:, 0:d]) * _dot(ya_ref[...], wa_ref[...])
    merged = merged + _sigmoid(g_ref[:, d:2 * d]) * _dot(yb_ref[...], wb_ref[...])
    merged = merged + _sigmoid(g_ref[:, 2 * d:3 * d]) * _dot(yc_ref[...], wc_ref[...])
    merged = merged + _sigmoid(g_ref[:, 3 * d:4 * d]) * _dot(yd_ref[...], wd_ref[...])
    o_ref[...] = x_ref[...] + g1_ref[0] * _dot(merged.astype(BF16), wo_ref[...])


def _merge(ya, yb, yc, yd, gates, x, mod, k_gate, wa, wb, wc, wd, wo, seq, tm):
    m, d = x.shape
    nb = m // seq
    rows = lambda i: (i, 0)
    const = lambda i: (0, 0)
    return pl.pallas_call(
        _merge_kernel,
        grid=(m // tm,),
        in_specs=[pl.BlockSpec((tm, ya.shape[1]), rows),
                  pl.BlockSpec((tm, yb.shape[1]), rows),
                  pl.BlockSpec((tm, yc.shape[1]), rows),
                  pl.BlockSpec((tm, yd.shape[1]), rows),
                  pl.BlockSpec((tm, N_BRANCH * d), rows),
                  pl.BlockSpec((tm, d), rows),
                  pl.BlockSpec((1, 1, d), lambda i: (k_gate * nb + (i * tm) // seq, 0, 0)),
                  pl.BlockSpec(wa.shape, const),
                  pl.BlockSpec(wb.shape, const),
                  pl.BlockSpec(wc.shape, const),
                  pl.BlockSpec(wd.shape, const),
                  pl.BlockSpec(wo.shape, const)],
        out_specs=pl.BlockSpec((tm, d), rows),
        out_shape=jax.ShapeDtypeStruct((m, d), F32),
        compiler_params=_cparams(("parallel",)),
        name="merge",
    )(ya, yb, yc, yd, gates, x, mod, wa, wb, wc, wd, wo)


def _ffn_kernel(x_ref, nw_ref, sc_ref, sh_ref, g_ref, w1_ref, w2_ref, o_ref, h_scr, acc_scr):
    j = pl.program_id(1)

    @pl.when(j == 0)
    def _():
        h_scr[...] = _rms_mod(x_ref[...], nw_ref[...], sc_ref[0], sh_ref[0]).astype(BF16)
        acc_scr[...] = jnp.zeros(acc_scr.shape, F32)

    a = jnp.maximum(_dot(h_scr[...], w1_ref[...]), 0.0)
    acc_scr[...] += _dot((a * a).astype(BF16), w2_ref[...])

    @pl.when(j == pl.num_programs(1) - 1)
    def _():
        o_ref[...] = x_ref[...] + g_ref[0] * acc_scr[...]


def _ffn(x, nw, mod, k_sc, k_sh, k_gate, w1, w2, seq, tm, tf):
    m, d = x.shape
    nb = m // seq
    dff = w1.shape[1]
    modrow = lambda k: (lambda i, j: (k * nb + (i * tm) // seq, 0, 0))
    return pl.pallas_call(
        _ffn_kernel,
        grid=(m // tm, dff // tf),
        in_specs=[pl.BlockSpec((tm, d), lambda i, j: (i, 0)),
                  pl.BlockSpec((1, d), lambda i, j: (0, 0)),
                  pl.BlockSpec((1, 1, d), modrow(k_sc)),
                  pl.BlockSpec((1, 1, d), modrow(k_sh)),
                  pl.BlockSpec((1, 1, d), modrow(k_gate)),
                  pl.BlockSpec((d, tf), lambda i, j: (0, j)),
                  pl.BlockSpec((tf, d), lambda i, j: (j, 0))],
        out_specs=pl.BlockSpec((tm, d), lambda i, j: (i, 0)),
        out_shape=jax.ShapeDtypeStruct((m, d), F32),
        scratch_shapes=[pltpu.VMEM((tm, d), BF16), pltpu.VMEM((tm, d), F32)],
        compiler_params=_cparams(("parallel", "arbitrary")),
        name="ffn",
    )(x, nw.reshape(1, d), mod, mod, mod, w1, w2)


def _pad_cols(w, width):
    return jnp.pad(w, ((0, 0), (0, width - w.shape[1])))


def _split_w_in(w):
    o = 0
    uv = w[:, o:o + 2 * GMLP_WIDTH]; o += 2 * GMLP_WIDTH
    fox_qkv = w[:, o:o + 3 * FOX_WIDTH]; o += 3 * FOX_WIDTH
    fox_f = w[:, o:o + FOX_HEADS]; o += FOX_HEADS
    moba_qk = w[:, o:o + 2 * MOBA_WIDTH]; o += 2 * MOBA_WIDTH
    moba_v = w[:, o:o + MOBA_WIDTH]; o += MOBA_WIDTH
    z = w[:, o:o + SSM_WIDTH]; o += SSM_WIDTH
    xbc = w[:, o:o + SSM_CONV_DIM]; o += SSM_CONV_DIM
    dt = w[:, o:o + SSM_HEADS]; o += SSM_HEADS
    gates = w[:, o:]
    w_attn = jnp.concatenate([fox_qkv, moba_v], axis=1)
    w_mqkf = jnp.concatenate([moba_qk, _pad_cols(fox_f, LANES)], axis=1)
    w_ssd = jnp.concatenate([xbc, z, _pad_cols(dt, LANES)], axis=1)
    return [t.astype(BF16) for t in (uv, w_attn, w_mqkf, w_ssd, gates)]


def _tile_rows(seq, want):
    return min(want, seq)


def kernel(x, c, ada_w, ada_b, norm_mix_w, w_in, gmlp_ln_w, gmlp_ln_b, gmlp_ws, gmlp_bs, fox_f_bias, ssm_conv_w, ssm_conv_b, ssm_dt_bias, ssm_a_log, ssm_d, ssm_norm_w, w_branch_a, w_branch_b, w_branch_c, w_branch_d, w_out, norm_mlp_w, mlp_w1, mlp_w2, final_norm_w):
    nb, seq, d = x.shape
    m = nb * seq
    depth = ada_w.shape[0]
    tm = _tile_rows(seq, 512)
    t_attn = _tile_rows(seq, 256)
    fox_pairs = FOX_HEADS // 2
    nblk = seq // MOBA_BLOCK
    slots = LANES // MOBA_HEADS
    assert seq % MOBA_BLOCK == 0 and nblk <= slots and seq % SSM_CHUNK == 0

    mod_all = _modulation(c, ada_w, ada_b)[:, :nb]
    mod_all = mod_all.reshape(depth, nb, 6, d).transpose(0, 2, 1, 3).reshape(depth, 6 * nb, 1, d)
    cos_t, sin_t = _rope_tables(seq)
    head_eye = jnp.eye(MOBA_HEADS, dtype=F32)

    xf = x.reshape(m, d)
    for l in range(depth):
        mod = mod_all[l]
        w_uv, w_attn, w_mqkf, w_ssd, w_gate = _split_w_in(w_in[l])
        h = _norm_mod(xf, norm_mix_w[l], mod, 1, 0, seq, tm)
        p_uv = _matmul(h, w_uv, F32, tm, w_uv.shape[1], "proj_uv")
        p_attn = _matmul(h, w_attn, BF16, tm, w_attn.shape[1], "proj_attn")
        p_mqkf = _matmul(h, w_mqkf, F32, tm, w_mqkf.shape[1], "proj_mqkf")
        p_ssd = _matmul(h, w_ssd, F32, tm, w_ssd.shape[1], "proj_ssd")
        p_gate = _matmul(h, w_gate, F32, tm, 2 * d, "proj_gate")

        y_a = _gmlp(p_uv, gmlp_ln_w[l], gmlp_ln_b[l], gmlp_ws[l], gmlp_bs[l], tm)

        cum = _forget_cumsum(p_mqkf, 2 * MOBA_WIDTH // LANES, fox_f_bias[l], seq)
        cum_t = cum[:, :FOX_HEADS].reshape(nb, seq // t_attn, t_attn, fox_pairs, 2)
        cum_t = cum_t.transpose(0, 3, 1, 4, 2).reshape(nb * fox_pairs, seq // t_attn, 2, t_attn)
        y_b = _fox_attention(p_attn, 0, fox_pairs, 2 * fox_pairs, cum_t, seq, t_attn)

        k_rot, k_mean = _moba_k_prep(p_mqkf, 1, cos_t, sin_t, seq)
        km = k_mean.reshape(nb, nblk, MOBA_HEADS, HEAD_DIM).transpose(0, 2, 3, 1)
        km = jnp.pad(km, ((0, 0), (0, 0), (0, 0), (0, slots - nblk)))
        km_mat = jnp.einsum('bhdn,hk->bhdkn', km, head_eye).reshape(nb, MOBA_WIDTH, LANES)
        q_rot, not_sel = _moba_q_prep(p_mqkf, 0, cos_t, sin_t, km_mat, seq)
        y_c = _moba_attention(q_rot, not_sel, k_rot, p_attn, 3 * fox_pairs, seq)

        y_d = _ssd(p_ssd, ssm_conv_w[l], ssm_conv_b[l], ssm_dt_bias[l], ssm_a_log[l], ssm_d[l],
                   ssm_norm_w[l], seq)

        xf = _merge(y_a, y_b, y_c, y_d, p_gate, xf, mod, 2,
                    w_branch_a[l].astype(BF16), w_branch_b[l].astype(BF16), w_branch_c[l].astype(BF16),
                    w_branch_d[l].astype(BF16), w_out[l].astype(BF16), seq, _tile_rows(seq, 256))
        xf = _ffn(xf, norm_mlp_w[l], mod, 4, 3, 5, mlp_w1[l].astype(BF16), mlp_w2[l].astype(BF16),
                  seq, _tile_rows(seq, 1024), 1024)
    return _final_norm(xf, final_norm_w, tm).reshape(nb, seq, d)
```

```python
import functools
import math

import jax
import jax.numpy as jnp
from jax import lax
from jax.experimental import pallas as pl
from jax.experimental.pallas import tpu as pltpu

F32 = jnp.float32
BF16 = jnp.bfloat16

HEAD_DIM = 64
NORM_EPS = 1e-6
GMLP_GROUPS = 8
GMLP_WIDTH = GMLP_GROUPS * HEAD_DIM
GMLP_CHUNK = 128
FOX_HEADS = 8
FOX_WIDTH = FOX_HEADS * HEAD_DIM
MOBA_HEADS = 8
MOBA_WIDTH = MOBA_HEADS * HEAD_DIM
MOBA_BLOCK = 256
MOBA_TOPK = 3
ROPE_THETA = 500000.0
ROPE_DIM = HEAD_DIM // 4
SSM_HEADS = 12
SSM_HEAD_DIM = 64
SSM_WIDTH = SSM_HEADS * SSM_HEAD_DIM
SSM_GROUPS = 2
SSM_STATE = 128
SSM_CONV = 4
SSM_CHUNK = 128
SSM_CONV_DIM = SSM_WIDTH + 2 * SSM_GROUPS * SSM_STATE
N_BRANCH = 4

LANES = 128
SUBLANES = 8
NEG_BIG = -1e30
VMEM_LIMIT = 48 * 1024 * 1024


def _cparams(sem):
    return pltpu.CompilerParams(dimension_semantics=sem, vmem_limit_bytes=VMEM_LIMIT)


def _sigmoid(x):
    return 1.0 / (1.0 + jnp.exp(-x))


def _softplus(x):
    return jnp.maximum(x, 0.0) + jnp.log(1.0 + jnp.exp(-jnp.abs(x)))


def _dot(a, b):
    return jnp.dot(a, b, preferred_element_type=F32)


def _dot_nt(a, b):
    return lax.dot_general(a, b, (((1,), (1,)), ((), ())), preferred_element_type=F32)


def _split3(x):
    hi = x.astype(BF16)
    r = x - hi.astype(F32)
    mid = r.astype(BF16)
    lo = (r - mid.astype(F32)).astype(BF16)
    return hi, mid, lo


def _tri_cumsum(x):
    t = x.shape[0]
    row = lax.broadcasted_iota(jnp.int32, (t, t), 0)
    col = lax.broadcasted_iota(jnp.int32, (t, t), 1)
    tri = jnp.where(row >= col, 1.0, 0.0).astype(BF16)
    hi, mid, lo = _split3(x)
    return _dot(tri, hi) + _dot(tri, mid) + _dot(tri, lo)


def _dot_f32(a, b):
    ah, am, al = _split3(a)
    bh, bm, bl = _split3(b)
    small = _dot(am, bm) + _dot(ah, bl) + _dot(al, bh)
    return _dot(ah, bh) + (_dot(ah, bm) + _dot(am, bh)) + small


def _mod_kernel(c_ref, w_ref, b_ref, o_ref):
    c = c_ref[...]
    ca = c * _sigmoid(c)
    o_ref[0] = _dot_f32(ca, w_ref[0]) + b_ref[0]


def _modulation(c, ada_w, ada_b):
    nb, d = c.shape
    nl = ada_w.shape[0]
    bp = -(-nb // SUBLANES) * SUBLANES
    cp = jnp.pad(c, ((0, bp - nb), (0, 0)))
    return pl.pallas_call(
        _mod_kernel,
        grid=(nl, 6),
        in_specs=[pl.BlockSpec((bp, d), lambda l, k: (0, 0)),
                  pl.BlockSpec((1, d, d), lambda l, k: (l, 0, k)),
                  pl.BlockSpec((1, 1, d), lambda l, k: (l, 0, k))],
        out_specs=pl.BlockSpec((1, bp, d), lambda l, k: (l, 0, k)),
        out_shape=jax.ShapeDtypeStruct((nl, bp, 6 * d), F32),
        compiler_params=_cparams(("parallel", "parallel")),
        name="adaln_mod",
    )(cp, ada_w, ada_b.reshape(nl, 1, 6 * d))


def _rms_mod(x, nw, sc, sh):
    ms = jnp.mean(x * x, axis=-1, keepdims=True)
    xn = x * lax.rsqrt(ms + NORM_EPS)
    return xn * nw * (1.0 + sc) + sh


def _norm_kernel(x_ref, nw_ref, sc_ref, sh_ref, o_ref):
    o_ref[...] = _rms_mod(x_ref[...], nw_ref[...], sc_ref[0], sh_ref[0]).astype(o_ref.dtype)


def _norm_mod(x, nw, mod, k_sc, k_sh, seq, tm):
    m, d = x.shape
    nb = m // seq
    return pl.pallas_call(
        _norm_kernel,
        grid=(m // tm,),
        in_specs=[pl.BlockSpec((tm, d), lambda i: (i, 0)),
                  pl.BlockSpec((1, d), lambda i: (0, 0)),
                  pl.BlockSpec((1, 1, d), lambda i: (k_sc * nb + (i * tm) // seq, 0, 0)),
                  pl.BlockSpec((1, 1, d), lambda i: (k_sh * nb + (i * tm) // seq, 0, 0))],
        out_specs=pl.BlockSpec((tm, d), lambda i: (i, 0)),
        out_shape=jax.ShapeDtypeStruct((m, d), BF16),
        compiler_params=_cparams(("parallel",)),
        name="norm_mod",
    )(x, nw.reshape(1, d), mod, mod)


def _final_norm_kernel(x_ref, nw_ref, o_ref):
    x = x_ref[...]
    ms = jnp.mean(x * x, axis=-1, keepdims=True)
    o_ref[...] = x * lax.rsqrt(ms + NORM_EPS) * nw_ref[...]


def _final_norm(x, nw, tm):
    m, d = x.shape
    return pl.pallas_call(
        _final_norm_kernel,
        grid=(m // tm,),
        in_specs=[pl.BlockSpec((tm, d), lambda i: (i, 0)),
                  pl.BlockSpec((1, d), lambda i: (0, 0))],
        out_specs=pl.BlockSpec((tm, d), lambda i: (i, 0)),
        out_shape=jax.ShapeDtypeStruct((m, d), F32),
        compiler_params=_cparams(("parallel",)),
        name="final_norm",
    )(x, nw.reshape(1, d))


def _mm_kernel(a_ref, wt_ref, o_ref):
    o_ref[...] = _dot_nt(a_ref[...], wt_ref[...]).astype(o_ref.dtype)


def _matmul(a, wt, out_dtype, tm, tn, name):
    m, k = a.shape
    n = wt.shape[0]
    return pl.pallas_call(
        _mm_kernel,
        grid=(n // tn, m // tm),
        in_specs=[pl.BlockSpec((tm, k), lambda j, i: (i, 0)),
                  pl.BlockSpec((tn, k), lambda j, i: (j, 0))],
        out_specs=pl.BlockSpec((tm, tn), lambda j, i: (i, j)),
        out_shape=jax.ShapeDtypeStruct((m, n), out_dtype),
        compiler_params=_cparams(("parallel", "parallel")),
        name=name,
    )(a, wt)


def _mm_t_kernel(a_ref, wt_ref, o_ref):
    o_ref[...] = _dot_nt(wt_ref[...], a_ref[...]).astype(o_ref.dtype)


def _matmul_t(a, wt, out_dtype, tm, name):
    m, k = a.shape
    n = wt.shape[0]
    return pl.pallas_call(
        _mm_t_kernel,
        grid=(m // tm,),
        in_specs=[pl.BlockSpec((tm, k), lambda i: (i, 0)),
                  pl.BlockSpec((n, k), lambda i: (0, 0))],
        out_specs=pl.BlockSpec((n, tm), lambda i: (0, i)),
        out_shape=jax.ShapeDtypeStruct((n, m), out_dtype),
        compiler_params=_cparams(("parallel",)),
        name=name,
    )(a, wt)


def _gelu_tanh(x):
    c = math.sqrt(2.0 / math.pi)
    return 0.5 * x * (1.0 + jnp.tanh(c * (x + 0.044715 * (x * x * x))))


def _gmlp_kernel(uv_ref, lnw_ref, lnb_ref, ws_ref, bs_ref, o_ref, *, tm):
    g = _gelu_tanh(uv_ref[...])
    u = g[:, :GMLP_WIDTH]
    v = g[:, GMLP_WIDTH:]
    mu = jnp.mean(v, axis=-1, keepdims=True)
    vc = v - mu
    var = jnp.mean(vc * vc, axis=-1, keepdims=True)
    vn = (vc * lax.rsqrt(var + NORM_EPS) * lnw_ref[...] + lnb_ref[...]).astype(BF16)
    t = GMLP_CHUNK
    row = lax.broadcasted_iota(jnp.int32, (t, t), 0)
    col = lax.broadcasted_iota(jnp.int32, (t, t), 1)
    causal = row >= col
    ws = [jnp.where(causal, ws_ref[i], 0.0).astype(BF16) for i in range(GMLP_GROUPS)]
    first_head = lax.broadcasted_iota(jnp.int32, (t, LANES), 1) < HEAD_DIM
    for c in range(tm // t):
        rows = slice(c * t, (c + 1) * t)
        for p in range(GMLP_WIDTH // LANES):
            cols = slice(p * LANES, (p + 1) * LANES)
            vp = vn[rows, cols]
            mixed = jnp.where(first_head, _dot(ws[2 * p], vp), _dot(ws[2 * p + 1], vp))
            o_ref[rows, cols] = (u[rows, cols] * (mixed + bs_ref[:, cols])).astype(o_ref.dtype)


def _gmlp(uv, ln_w, ln_b, ws, bs, tm):
    m = uv.shape[0]
    w = GMLP_WIDTH
    bs_full = jnp.repeat(bs.T, HEAD_DIM, axis=1)
    return pl.pallas_call(
        functools.partial(_gmlp_kernel, tm=tm),
        grid=(m // tm,),
        in_specs=[pl.BlockSpec((tm, 2 * w), lambda i: (i, 0)),
                  pl.BlockSpec((1, w), lambda i: (0, 0)),
                  pl.BlockSpec((1, w), lambda i: (0, 0)),
                  pl.BlockSpec((GMLP_GROUPS, GMLP_CHUNK, GMLP_CHUNK), lambda i: (0, 0, 0)),
                  pl.BlockSpec((GMLP_CHUNK, w), lambda i: (0, 0))],
        out_specs=pl.BlockSpec((tm, w), lambda i: (i, 0)),
        out_shape=jax.ShapeDtypeStruct((m, w), BF16),
        compiler_params=_cparams(("parallel",)),
        name="gmlp",
    )(uv, ln_w.reshape(1, w), ln_b.reshape(1, w), ws, bs_full)


def _fcum_kernel(f_ref, b_ref, o_ref, *, seq):
    t = LANES
    r = lax.broadcasted_iota(jnp.int32, (LANES, LANES), 0)
    c = lax.broadcasted_iota(jnp.int32, (LANES, LANES), 1)
    spread = [jnp.where((c == 3 * r + part) & (r < FOX_HEADS), 1.0, 0.0).astype(BF16) for part in range(3)]

    def body(i, carry):
        r0 = pl.multiple_of(i * t, t)
        z = f_ref[pl.ds(r0, t), :] + b_ref[...]
        logf = jnp.minimum(z, 0.0) - jnp.log(1.0 + jnp.exp(-jnp.abs(z)))
        cs = _tri_cumsum(logf) + carry
        hi, mid, lo = _split3(-cs)
        o_ref[pl.ds(r0, t), :] = (_dot(hi, spread[0]) + _dot(mid, spread[1]) + _dot(lo, spread[2])).astype(o_ref.dtype)
        return cs[t - 1:t, :]

    lax.fori_loop(0, seq // t, body, jnp.zeros((1, LANES), F32))


def _forget_cumsum(proj, col_block, f_bias, seq):
    m = proj.shape[0]
    fb = jnp.pad(f_bias, (0, LANES - FOX_HEADS)).reshape(1, LANES)
    return pl.pallas_call(
        functools.partial(_fcum_kernel, seq=seq),
        grid=(m // seq,),
        in_specs=[pl.BlockSpec((seq, LANES), lambda b: (b, col_block)),
                  pl.BlockSpec((1, LANES), lambda b: (0, 0))],
        out_specs=pl.BlockSpec((seq, LANES), lambda b: (b, 0)),
        out_shape=jax.ShapeDtypeStruct((m, LANES), BF16),
        compiler_params=_cparams(("parallel",)),
        name="fox_cumsum",
    )(proj, fb)


def _attn_kernel(*refs, t, nblk, mode):
    if mode == "fox":
        q_ref, k_ref, ka_ref, vt_ref, o_ref, vp_scr, acc_scr, st_scr, p_scr = refs
    else:
        q_ref, ns_ref, k_ref, vt_ref, o_ref, vp_scr, acc_scr, st_scr, p_scr = refs
    p = pl.program_id(1)
    qi = pl.program_id(2)
    half = LANES // 2

    @pl.when(qi == 0)
    def _():
        chan = lax.broadcasted_iota(jnp.int32, (LANES, t), 0)
        for jb in range(nblk):
            vt = vt_ref[:, jb * t:(jb + 1) * t]
            ones = jnp.ones_like(vt)
            vp_scr[0, jb] = jnp.where(chan < half, vt, ones)
            vp_scr[1, jb] = jnp.where(chan >= half, vt, ones)

    lane = lax.broadcasted_iota(jnp.int32, (t, LANES), 1)
    key_i = lax.broadcasted_iota(jnp.int32, (t, t), 0)
    qry_i = lax.broadcasted_iota(jnp.int32, (t, t), 1)
    causal = key_i <= qry_i
    q = q_ref[...]
    if mode == "fox":
        q = q * (HEAD_DIM ** -0.5)
    slots = LANES // MOBA_HEADS
    qc = []
    for h in range(2):
        head_lanes = (lane < half) if h == 0 else (lane >= half)
        qh = jnp.where(head_lanes, q, jnp.zeros_like(q))
        if mode == "fox":
            first = 3 * (2 * p + h)
            aux = jnp.where((lane >= first) & (lane < first + 3), 1.0, 0.0).astype(BF16)
        else:
            aux = ns_ref[...]
        qc.append(jnp.concatenate([qh, aux], axis=1))

    def scores(j):
        r0 = pl.multiple_of(j * t, t)
        kb = k_ref[pl.ds(r0, t), :]
        out = []
        for h in range(2):
            if mode == "fox":
                ka = ka_ref[pl.ds(r0, t), :]
            else:
                ka = jnp.where(lane == (2 * p + h) * slots + j, NEG_BIG, 0.0).astype(BF16)
            out.append(_dot_nt(jnp.concatenate([kb, ka], axis=1), qc[h]))
        return out

    def softmax_step(ms, masked):
        ms2, alphas, pts = [], [], []
        for h in range(2):
            st = st_scr[h]
            if masked:
                st = jnp.where(causal, st, NEG_BIG)
            m_new = jnp.maximum(ms[h], jnp.max(st, axis=0, keepdims=True))
            alphas.append(jnp.exp(ms[h] - m_new))
            pts.append(jnp.exp(st - m_new).astype(BF16))
            ms2.append(m_new)
        return tuple(ms2), alphas, pts

    acc_scr[...] = jnp.zeros(acc_scr.shape, F32)
    p_scr[...] = jnp.zeros(p_scr.shape, BF16)
    first_scores = scores(0)
    for h in range(2):
        st_scr[h] = first_scores[h]
    m0 = jnp.full((1, t), -jnp.inf, F32)

    def body(j, ms):
        nxt = scores(j + 1)
        prev = jnp.maximum(j - 1, 0)
        pv = [_dot(vp_scr[h, prev], p_scr[h]) for h in range(2)]
        ms2, alphas, pts = softmax_step(ms, False)
        for h in range(2):
            acc_scr[h] = alphas[h] * (acc_scr[h] + pv[h])
            p_scr[h] = pts[h]
            st_scr[h] = nxt[h]
        return ms2

    ms = lax.fori_loop(0, qi, body, (m0, m0))
    prev = jnp.maximum(qi - 1, 0)
    pv = [_dot(vp_scr[h, prev], p_scr[h]) for h in range(2)]
    _, alphas, pts = softmax_step(ms, True)
    outs = []
    for h in range(2):
        a = alphas[h] * (acc_scr[h] + pv[h]) + _dot(vp_scr[h, qi], pts[h])
        denom = a[(1 - h) * half:(1 - h) * half + 1, :]
        outs.append(a[h * half:(h + 1) * half, :] / denom)
    o_ref[...] = jnp.concatenate(outs, axis=0).T.astype(o_ref.dtype)


def _block_attention(mode, q_arr, q_col, aux_arr, k_arr, k_col, vt_arr, vt_row, seq, t):
    m = q_arr.shape[0]
    nb = m // seq
    nq = seq // t
    pairs = FOX_HEADS // 2
    q_spec = pl.BlockSpec((t, LANES), lambda b, p, i: (b * nq + i, q_col + p))
    k_spec = pl.BlockSpec((seq, LANES), lambda b, p, i: (b, k_col + p))
    vt_spec = pl.BlockSpec((LANES, seq), lambda b, p, i: (vt_row + p, b))
    if mode == "fox":
        in_specs = [q_spec, k_spec, pl.BlockSpec((seq, LANES), lambda b, p, i: (b, 0)), vt_spec]
        args = (q_arr, k_arr, aux_arr, vt_arr)
    else:
        in_specs = [q_spec, pl.BlockSpec((t, LANES), lambda b, p, i: (b * nq + i, 0)), k_spec, vt_spec]
        args = (q_arr, aux_arr, k_arr, vt_arr)
    return pl.pallas_call(
        functools.partial(_attn_kernel, t=t, nblk=nq, mode=mode),
        grid=(nb, pairs, nq),
        in_specs=in_specs,
        out_specs=pl.BlockSpec((t, LANES), lambda b, p, i: (b * nq + i, p)),
        out_shape=jax.ShapeDtypeStruct((m, pairs * LANES), BF16),
        scratch_shapes=[pltpu.VMEM((2, nq, LANES, t), BF16), pltpu.VMEM((2, LANES, t), F32),
                        pltpu.VMEM((2, t, t), F32), pltpu.VMEM((2, t, t), BF16)],
        compiler_params=_cparams(("arbitrary", "arbitrary", "arbitrary")),
        name=mode + "_attn",
    )(*args)


def _rope_tables(seq):
    half = ROPE_DIM // 2
    inv_freq = ROPE_THETA ** (-jnp.arange(half, dtype=F32) / half)
    ang = jnp.arange(seq, dtype=F32)[:, None] * inv_freq[None, :]
    cos, sin = jnp.cos(ang), jnp.sin(ang)
    ones = jnp.ones((seq, HEAD_DIM - ROPE_DIM), F32)
    cos_h = jnp.concatenate([cos, cos, ones], axis=1)
    sin_h = jnp.concatenate([-sin, sin, 0.0 * ones], axis=1)
    return jnp.tile(cos_h, (1, 2)), jnp.tile(sin_h, (1, 2))


def _rotary(x, cos_t, sin_t):
    half = ROPE_DIM // 2
    lane = lax.broadcasted_iota(jnp.int32, (x.shape[0], LANES), 1)
    first_half = (lane & (HEAD_DIM - 1)) < half
    outs = []
    for j in range(x.shape[1] // LANES):
        xc = x[:, j * LANES:(j + 1) * LANES]
        up = pltpu.roll(xc, LANES - half, axis=1)
        down = pltpu.roll(xc, half, axis=1)
        outs.append(xc * cos_t + jnp.where(first_half, up, down) * sin_t)
    return jnp.concatenate(outs, axis=1)


def _moba_k_kernel(k_ref, cos_ref, sin_ref, ko_ref, km_ref):
    kr = _rotary(k_ref[...], cos_ref[...], sin_ref[...])
    ko_ref[...] = kr.astype(ko_ref.dtype)
    km_ref[0] = jnp.mean(kr, axis=0, keepdims=True)


def _moba_k_prep(proj, k_col, cos_t, sin_t, seq):
    m = proj.shape[0]
    t = MOBA_BLOCK
    nblk = seq // t
    return pl.pallas_call(
        _moba_k_kernel,
        grid=(m // t,),
        in_specs=[pl.BlockSpec((t, MOBA_WIDTH), lambda i: (i, k_col)),
                  pl.BlockSpec((t, LANES), lambda i: (i % nblk, 0)),
                  pl.BlockSpec((t, LANES), lambda i: (i % nblk, 0))],
        out_specs=[pl.BlockSpec((t, MOBA_WIDTH), lambda i: (i, 0)),
                   pl.BlockSpec((1, 1, MOBA_WIDTH), lambda i: (i, 0, 0))],
        out_shape=[jax.ShapeDtypeStruct((m, MOBA_WIDTH), BF16),
                   jax.ShapeDtypeStruct((m // t, 1, MOBA_WIDTH), F32)],
        compiler_params=_cparams(("parallel",)),
        name="moba_k_prep",
    )(proj, cos_t, sin_t)


def _moba_q_kernel(q_ref, cos_ref, sin_ref, km_ref, qo_ref, ns_ref, *, nblk):
    own = pl.program_id(0) % nblk
    qr = _rotary(q_ref[...], cos_ref[...], sin_ref[...])
    qo_ref[...] = (qr * (HEAD_DIM ** -0.5)).astype(qo_ref.dtype)
    gate = _dot_f32(qr, km_ref[0])
    slots = LANES // MOBA_HEADS
    lane = lax.broadcasted_iota(jnp.int32, gate.shape, 1)
    n = lane & (slots - 1)
    cnt = jnp.zeros(gate.shape, jnp.int32)
    for r in range(1, slots):
        wrapped = (n + r) >= slots
        other = jnp.where(wrapped, pltpu.roll(gate, slots - r, axis=1), pltpu.roll(gate, LANES - r, axis=1))
        other_n = jnp.where(wrapped, n + (r - slots), n + r)
        beats = (other > gate) | ((other == gate) & wrapped)
        cnt = cnt + jnp.where((other_n < own) & beats, 1, 0)
    selected = (n == own) | ((n < own) & (cnt < MOBA_TOPK))
    ns_ref[...] = jnp.where(selected, 0.0, 1.0).astype(ns_ref.dtype)


def _moba_q_prep(proj, q_col, cos_t, sin_t, km_mat, seq):
    m = proj.shape[0]
    t = MOBA_BLOCK
    nblk = seq // t
    return pl.pallas_call(
        functools.partial(_moba_q_kernel, nblk=nblk),
        grid=(m // t,),
        in_specs=[pl.BlockSpec((t, MOBA_WIDTH), lambda i: (i, q_col)),
                  pl.BlockSpec((t, LANES), lambda i: (i % nblk, 0)),
                  pl.BlockSpec((t, LANES), lambda i: (i % nblk, 0)),
                  pl.BlockSpec((1, MOBA_WIDTH, LANES), lambda i: (i // nblk, 0, 0))],
        out_specs=[pl.BlockSpec((t, MOBA_WIDTH), lambda i: (i, 0)),
                   pl.BlockSpec((t, LANES), lambda i: (i, 0))],
        out_shape=[jax.ShapeDtypeStruct((m, MOBA_WIDTH), BF16),
                   jax.ShapeDtypeStruct((m, LANES), BF16)],
        compiler_params=_cparams(("parallel",)),
        name="moba_q_prep",
    )(proj, cos_t, sin_t, km_mat)


def _pair_cols(x, h0, h1, shape):
    lane = lax.broadcasted_iota(jnp.int32, shape, 1)
    a = jnp.broadcast_to(x[:, h0:h0 + 1], shape)
    b = jnp.broadcast_to(x[:, h1:h1 + 1], shape)
    return jnp.where(lane < SSM_HEAD_DIM, a, b)


def _ssd_kernel(in_ref, cw_ref, cb_ref, dtb_ref, alog_ref, dsk_ref, nw_ref, o_ref, xpad_scr, st_scr):
    t = SSM_CHUNK
    c = pl.program_id(1)
    halo = SUBLANES

    @pl.when(c == 0)
    def _():
        xpad_scr[0:halo, :] = jnp.zeros((halo, SSM_CONV_DIM), F32)
        st_scr[...] = jnp.zeros(st_scr.shape, F32)

    xpad_scr[halo:halo + t, :] = in_ref[:, 0:SSM_CONV_DIM]
    conv = cb_ref[...] + cw_ref[0:1, :] * xpad_scr[pl.ds(halo - SSM_CONV + 1, t), :]
    for k in range(1, SSM_CONV):
        conv = conv + cw_ref[k:k + 1, :] * xpad_scr[pl.ds(halo - SSM_CONV + 1 + k, t), :]
    xpad_scr[0:halo, :] = xpad_scr[t:t + halo, :]
    xbc = conv * _sigmoid(conv)
    z = in_ref[:, SSM_CONV_DIM:SSM_CONV_DIM + SSM_WIDTH]
    dt = _softplus(in_ref[:, SSM_CONV_DIM + SSM_WIDTH:] + dtb_ref[...])
    a = -jnp.exp(alog_ref[...])
    acum = _tri_cumsum(dt * a)
    acum_t = acum.T
    tot = acum[t - 1:t, :]
    decay_end = jnp.exp(tot - acum)
    decay_in = jnp.exp(acum)
    exp_tot = jnp.exp(tot)
    row = lax.broadcasted_iota(jnp.int32, (t, t), 0)
    col = lax.broadcasted_iota(jnp.int32, (t, t), 1)
    causal = row >= col
    lane = lax.broadcasted_iota(jnp.int32, (t, LANES), 1)
    lane1 = lax.broadcasted_iota(jnp.int32, (1, LANES), 1)
    pair_shape = (t, LANES)
    rep = SSM_HEADS // SSM_GROUPS
    b_off = SSM_WIDTH
    c_off = SSM_WIDTH + SSM_GROUPS * SSM_STATE
    ys = []
    for g in range(SSM_GROUPS):
        bm = xbc[:, b_off + g * SSM_STATE:b_off + (g + 1) * SSM_STATE]
        cm = xbc[:, c_off + g * SSM_STATE:c_off + (g + 1) * SSM_STATE].astype(BF16)
        cb = _dot_nt(cm, bm.astype(BF16))
        bm_t = bm.T.astype(BF16)
        for pr in range(rep // 2):
            j = g * (rep // 2) + pr
            h0, h1 = 2 * j, 2 * j + 1
            xp = xbc[:, j * LANES:(j + 1) * LANES]
            xdt = xp * _pair_cols(dt, h0, h1, pair_shape)
            xdt_b = xdt.astype(BF16)
            yd = []
            for h in (h0, h1):
                diff = acum[:, h:h + 1] - acum_t[h:h + 1, :]
                w = cb * jnp.exp(jnp.where(causal, diff, NEG_BIG))
                yd.append(_dot(w.astype(BF16), xdt_b))
            y_diag = jnp.where(lane < SSM_HEAD_DIM, yd[0], yd[1])
            prev = st_scr[j]
            y_off = _dot(cm, prev.astype(BF16)) * _pair_cols(decay_in, h0, h1, pair_shape)
            xw = (xdt * _pair_cols(decay_end, h0, h1, pair_shape)).astype(BF16)
            scale = jnp.where(lane1 < SSM_HEAD_DIM, exp_tot[:, h0:h0 + 1], exp_tot[:, h1:h1 + 1])
            st_scr[j] = prev * scale + _dot(bm_t, xw)
            ys.append(y_diag + y_off + dsk_ref[:, j * LANES:(j + 1) * LANES] * xp)
    y = jnp.concatenate(ys, axis=1)
    y = y * (z * _sigmoid(z))
    gw = SSM_WIDTH // SSM_GROUPS
    outs = []
    for g in range(SSM_GROUPS):
        seg = y[:, g * gw:(g + 1) * gw]
        ms = jnp.mean(seg * seg, axis=-1, keepdims=True)
        outs.append(seg * lax.rsqrt(ms + NORM_EPS) * nw_ref[:, g * gw:(g + 1) * gw])
    o_ref[...] = jnp.concatenate(outs, axis=1).astype(o_ref.dtype)


def _ssd(proj, conv_w, conv_b, dt_bias, a_log, d_skip, norm_w, seq):
    m, wtot = proj.shape
    t = SSM_CHUNK
    nc = seq // t
    pad = LANES - SSM_HEADS
    dtb = jnp.pad(dt_bias, (0, pad)).reshape(1, LANES)
    alog = jnp.pad(a_log, (0, pad)).reshape(1, LANES)
    dsk = jnp.repeat(d_skip, SSM_HEAD_DIM).reshape(1, SSM_WIDTH)
    const = lambda b, c: (0, 0)
    return pl.pallas_call(
        _ssd_kernel,
        grid=(m // seq, nc),
        in_specs=[pl.BlockSpec((t, wtot), lambda b, c: (b * nc + c, 0)),
                  pl.BlockSpec((SSM_CONV, SSM_CONV_DIM), const),
                  pl.BlockSpec((1, SSM_CONV_DIM), const),
                  pl.BlockSpec((1, LANES), const),
                  pl.BlockSpec((1, LANES), const),
                  pl.BlockSpec((1, SSM_WIDTH), const),
                  pl.BlockSpec((1, SSM_WIDTH), const)],
        out_specs=pl.BlockSpec((t, SSM_WIDTH), lambda b, c: (b * nc + c, 0)),
        out_shape=jax.ShapeDtypeStruct((m, SSM_WIDTH), BF16),
        scratch_shapes=[pltpu.VMEM((t + SUBLANES, SSM_CONV_DIM), F32),
                        pltpu.VMEM((SSM_HEADS // 2, SSM_STATE, LANES), F32)],
        compiler_params=_cparams(("parallel", "arbitrary")),
        name="ssd",
    )(proj, conv_w, conv_b.reshape(1, SSM_CONV_DIM), dtb, alog, dsk, norm_w.reshape(1, SSM_WIDTH))


def _merge_kernel(ya_ref, yb_ref, yc_ref, yd_ref, g_ref, x_ref, g1_ref,
                  wa_ref, wb_ref, wc_ref, wd_ref, wo_ref, o_ref):
    d = x_ref.shape[1]
    merged = _sigmoid(g_ref[:, 0:d]) * _dot(ya_ref[...], wa_ref[...])
    merged = merged + _sigmoid(g_ref[:, d:2 * d]) * _dot(yb_ref[...], wb_ref[...])
    merged = merged + _sigmoid(g_ref[:, 2 * d:3 * d]) * _dot(yc_ref[...], wc_ref[...])
    merged = merged + _sigmoid(g_ref[:, 3 * d:4 * d]) * _dot(yd_ref[...], wd_ref[...])
    o_ref[...] = x_ref[...] + g1_ref[0] * _dot(merged.astype(BF16), wo_ref[...])


def _merge(ya, yb, yc, yd, gates, x, mod, k_gate, wa, wb, wc, wd, wo, seq, tm):
    m, d = x.shape
    nb = m // seq
    rows = lambda i: (i, 0)
    const = lambda i: (0, 0)
    return pl.pallas_call(
        _merge_kernel,
        grid=(m // tm,),
        in_specs=[pl.BlockSpec((tm, ya.shape[1]), rows),
                  pl.BlockSpec((tm, yb.shape[1]), rows),
                  pl.BlockSpec((tm, yc.shape[1]), rows),
                  pl.BlockSpec((tm, yd.shape[1]), rows),
                  pl.BlockSpec((tm, N_BRANCH * d), rows),
                  pl.BlockSpec((tm, d), rows),
                  pl.BlockSpec((1, 1, d), lambda i: (k_gate * nb + (i * tm) // seq, 0, 0)),
                  pl.BlockSpec(wa.shape, const),
                  pl.BlockSpec(wb.shape, const),
                  pl.BlockSpec(wc.shape, const),
                  pl.BlockSpec(wd.shape, const),
                  pl.BlockSpec(wo.shape, const)],
        out_specs=pl.BlockSpec((tm, d), rows),
        out_shape=jax.ShapeDtypeStruct((m, d), F32),
        compiler_params=_cparams(("parallel",)),
        name="merge",
    )(ya, yb, yc, yd, gates, x, mod, wa, wb, wc, wd, wo)


def _ffn_kernel(x_ref, nw_ref, sc_ref, sh_ref, g_ref, w1_ref, w2_ref, o_ref, h_scr, acc_scr):
    j = pl.program_id(1)

    @pl.when(j == 0)
    def _():
        h_scr[...] = _rms_mod(x_ref[...], nw_ref[...], sc_ref[0], sh_ref[0]).astype(BF16)
        acc_scr[...] = jnp.zeros(acc_scr.shape, F32)

    a = jnp.maximum(_dot(h_scr[...], w1_ref[...]), 0.0)
    acc_scr[...] += _dot((a * a).astype(BF16), w2_ref[...])

    @pl.when(j == pl.num_programs(1) - 1)
    def _():
        o_ref[...] = x_ref[...] + g_ref[0] * acc_scr[...]


def _ffn(x, nw, mod, k_sc, k_sh, k_gate, w1, w2, seq, tm, tf):
    m, d = x.shape
    nb = m // seq
    dff = w1.shape[1]
    modrow = lambda k: (lambda i, j: (k * nb + (i * tm) // seq, 0, 0))
    return pl.pallas_call(
        _ffn_kernel,
        grid=(m // tm, dff // tf),
        in_specs=[pl.BlockSpec((tm, d), lambda i, j: (i, 0)),
                  pl.BlockSpec((1, d), lambda i, j: (0, 0)),
                  pl.BlockSpec((1, 1, d), modrow(k_sc)),
                  pl.BlockSpec((1, 1, d), modrow(k_sh)),
                  pl.BlockSpec((1, 1, d), modrow(k_gate)),
                  pl.BlockSpec((d, tf), lambda i, j: (0, j)),
                  pl.BlockSpec((tf, d), lambda i, j: (j, 0))],
        out_specs=pl.BlockSpec((tm, d), lambda i, j: (i, 0)),
        out_shape=jax.ShapeDtypeStruct((m, d), F32),
        scratch_shapes=[pltpu.VMEM((tm, d), BF16), pltpu.VMEM((tm, d), F32)],
        compiler_params=_cparams(("parallel", "arbitrary")),
        name="ffn",
    )(x, nw.reshape(1, d), mod, mod, mod, w1, w2)


def _pad_rows(w, rows):
    return jnp.pad(w, ((0, rows - w.shape[0]), (0, 0)))


def _split_w_in(w_in, l):
    wt = jnp.transpose(w_in, (2, 0, 1))[:, l, :]
    o = 0
    uv = wt[o:o + 2 * GMLP_WIDTH]; o += 2 * GMLP_WIDTH
    fox_qk = wt[o:o + 2 * FOX_WIDTH]; o += 2 * FOX_WIDTH
    fox_v = wt[o:o + FOX_WIDTH]; o += FOX_WIDTH
    fox_f = wt[o:o + FOX_HEADS]; o += FOX_HEADS
    moba_qk = wt[o:o + 2 * MOBA_WIDTH]; o += 2 * MOBA_WIDTH
    moba_v = wt[o:o + MOBA_WIDTH]; o += MOBA_WIDTH
    z = wt[o:o + SSM_WIDTH]; o += SSM_WIDTH
    xbc = wt[o:o + SSM_CONV_DIM]; o += SSM_CONV_DIM
    dt = wt[o:o + SSM_HEADS]; o += SSM_HEADS
    gates = wt[o:]
    w_v = jnp.concatenate([fox_v, moba_v], axis=0)
    w_mqkf = jnp.concatenate([moba_qk, _pad_rows(fox_f, LANES)], axis=0)
    w_ssd = jnp.concatenate([xbc, z, _pad_rows(dt, LANES)], axis=0)
    return [g.astype(BF16) for g in (uv, fox_qk, w_v, w_mqkf, w_ssd, gates)]


def _tile_rows(seq, want):
    return min(want, seq)


def kernel(x, c, ada_w, ada_b, norm_mix_w, w_in, gmlp_ln_w, gmlp_ln_b, gmlp_ws, gmlp_bs, fox_f_bias, ssm_conv_w, ssm_conv_b, ssm_dt_bias, ssm_a_log, ssm_d, ssm_norm_w, w_branch_a, w_branch_b, w_branch_c, w_branch_d, w_out, norm_mlp_w, mlp_w1, mlp_w2, final_norm_w):
    nb, seq, d = x.shape
    m = nb * seq
    depth = ada_w.shape[0]
    tm = _tile_rows(seq, 512)
    t_attn = _tile_rows(seq, 256)
    pairs = FOX_HEADS // 2
    nblk = seq // MOBA_BLOCK
    slots = LANES // MOBA_HEADS
    assert seq % MOBA_BLOCK == 0 and nblk <= slots and seq % SSM_CHUNK == 0

    mod_all = _modulation(c, ada_w, ada_b)[:, :nb]
    mod_all = mod_all.reshape(depth, nb, 6, d).transpose(0, 2, 1, 3).reshape(depth, 6 * nb, 1, d)
    cos_t, sin_t = _rope_tables(seq)
    head_eye = jnp.eye(MOBA_HEADS, dtype=F32)

    xf = x.reshape(m, d)
    for l in range(depth):
        mod = mod_all[l]
        w_uv, w_fqk, w_v, w_mqkf, w_ssd, w_gate = _split_w_in(w_in, l)
        h = _norm_mod(xf, norm_mix_w[l], mod, 1, 0, seq, tm)
        p_uv = _matmul(h, w_uv, F32, tm, w_uv.shape[0], "proj_uv")
        p_fqk = _matmul(h, w_fqk, BF16, tm, w_fqk.shape[0], "proj_fox_qk")
        p_vt = _matmul_t(h, w_v, BF16, tm, "proj_vt")
        p_mqkf = _matmul(h, w_mqkf, F32, tm, w_mqkf.shape[0], "proj_mqkf")
        p_ssd = _matmul(h, w_ssd, F32, tm, w_ssd.shape[0], "proj_ssd")
        p_gate = _matmul(h, w_gate, F32, tm, 2 * d, "proj_gate")

        y_a = _gmlp(p_uv, gmlp_ln_w[l], gmlp_ln_b[l], gmlp_ws[l], gmlp_bs[l], tm)

        k_aux = _forget_cumsum(p_mqkf, 2 * MOBA_WIDTH // LANES, fox_f_bias[l], seq)
        y_b = _block_attention("fox", p_fqk, 0, k_aux, p_fqk, pairs, p_vt, 0, seq, t_attn)

        k_rot, k_mean = _moba_k_prep(p_mqkf, 1, cos_t, sin_t, seq)
        km = k_mean.reshape(nb, nblk, MOBA_HEADS, HEAD_DIM).transpose(0, 2, 3, 1)
        km = jnp.pad(km, ((0, 0), (0, 0), (0, 0), (0, slots - nblk)))
        km_mat = jnp.einsum('bhdn,hk->bhdkn', km, head_eye).reshape(nb, MOBA_WIDTH, LANES)
        q_rot, not_sel = _moba_q_prep(p_mqkf, 0, cos_t, sin_t, km_mat, seq)
        y_c = _block_attention("moba", q_rot, 0, not_sel, k_rot, 0, p_vt, pairs, seq, MOBA_BLOCK)

        y_d = _ssd(p_ssd, ssm_conv_w[l], ssm_conv_b[l], ssm_dt_bias[l], ssm_a_log[l], ssm_d[l],
                   ssm_norm_w[l], seq)

        xf = _merge(y_a, y_b, y_c, y_d, p_gate, xf, mod, 2,
                    w_branch_a[l].astype(BF16), w_branch_b[l].astype(BF16), w_branch_c[l].astype(BF16),
                    w_branch_d[l].astype(BF16), w_out[l].astype(BF16), seq, _tile_rows(seq, 256))
        xf = _ffn(xf, norm_mlp_w[l], mod, 4, 3, 5, mlp_w1[l].astype(BF16), mlp_w2[l].astype(BF16),
                  seq, _tile_rows(seq, 1024), 1024)
    return _final_norm(xf, final_norm_w, tm).reshape(nb, seq, d)
```

```python
import functools
import math

import jax
import jax.numpy as jnp
from jax import lax
from jax.experimental import pallas as pl
from jax.experimental.pallas import tpu as pltpu

F32 = jnp.float32
BF16 = jnp.bfloat16

HEAD_DIM = 64
NORM_EPS = 1e-6
GMLP_GROUPS = 8
GMLP_WIDTH = GMLP_GROUPS * HEAD_DIM
GMLP_CHUNK = 128
FOX_HEADS = 8
FOX_WIDTH = FOX_HEADS * HEAD_DIM
MOBA_HEADS = 8
MOBA_WIDTH = MOBA_HEADS * HEAD_DIM
MOBA_BLOCK = 256
MOBA_TOPK = 3
ROPE_THETA = 500000.0
ROPE_DIM = HEAD_DIM // 4
SSM_HEADS = 12
SSM_HEAD_DIM = 64
SSM_WIDTH = SSM_HEADS * SSM_HEAD_DIM
SSM_GROUPS = 2
SSM_STATE = 128
SSM_CONV = 4
SSM_CHUNK = 128
SSM_CONV_DIM = SSM_WIDTH + 2 * SSM_GROUPS * SSM_STATE
N_BRANCH = 4

LANES = 128
SUBLANES = 8
NEG_BIG = -1e30
VMEM_LIMIT = 48 * 1024 * 1024
ATTN_Q_TILE = 512


def _cparams(sem):
    return pltpu.CompilerParams(dimension_semantics=sem, vmem_limit_bytes=VMEM_LIMIT)


def _sigmoid(x):
    return 1.0 / (1.0 + jnp.exp(-x))


def _softplus(x):
    return jnp.maximum(x, 0.0) + jnp.log(1.0 + jnp.exp(-jnp.abs(x)))


def _dot(a, b):
    return jnp.dot(a, b, preferred_element_type=F32)


def _dot_nt(a, b):
    return lax.dot_general(a, b, (((1,), (1,)), ((), ())), preferred_element_type=F32)


def _keep_high_half(x):
    bits = lax.bitcast_convert_type(x, jnp.uint32) & jnp.uint32(0xFFFF0000)
    return lax.bitcast_convert_type(bits, F32)


def _split3(x):
    hi = _keep_high_half(x)
    r = x - hi
    mid = _keep_high_half(r)
    lo = r - mid
    return hi.astype(BF16), mid.astype(BF16), lo.astype(BF16)


def _tri_cumsum(x):
    t = x.shape[0]
    row = lax.broadcasted_iota(jnp.int32, (t, t), 0)
    col = lax.broadcasted_iota(jnp.int32, (t, t), 1)
    tri = jnp.where(row >= col, 1.0, 0.0).astype(BF16)
    hi, mid, lo = _split3(x)
    return _dot(tri, hi) + _dot(tri, mid) + _dot(tri, lo)


def _dot_f32(a, b):
    ah, am, al = _split3(a)
    bh, bm, bl = _split3(b)
    small = _dot(am, bm) + _dot(ah, bl) + _dot(al, bh)
    return _dot(ah, bh) + (_dot(ah, bm) + _dot(am, bh)) + small


def _mod_kernel(c_ref, w_ref, b_ref, o_ref):
    c = c_ref[...]
    ca = c * _sigmoid(c)
    o_ref[0] = _dot_f32(ca, w_ref[0]) + b_ref[0]


def _modulation(c, ada_w, ada_b):
    nb, d = c.shape
    nl = ada_w.shape[0]
    bp = -(-nb // SUBLANES) * SUBLANES
    cp = jnp.pad(c, ((0, bp - nb), (0, 0)))
    return pl.pallas_call(
        _mod_kernel,
        grid=(nl, 6),
        in_specs=[pl.BlockSpec((bp, d), lambda l, k: (0, 0)),
                  pl.BlockSpec((1, d, d), lambda l, k: (l, 0, k)),
                  pl.BlockSpec((1, 1, d), lambda l, k: (l, 0, k))],
        out_specs=pl.BlockSpec((1, bp, d), lambda l, k: (l, 0, k)),
        out_shape=jax.ShapeDtypeStruct((nl, bp, 6 * d), F32),
        compiler_params=_cparams(("parallel", "parallel")),
        name="adaln_mod",
    )(cp, ada_w, ada_b.reshape(nl, 1, 6 * d))


def _rms_mod(x, nw, sc, sh):
    ms = jnp.mean(x * x, axis=-1, keepdims=True)
    xn = x * lax.rsqrt(ms + NORM_EPS)
    return xn * nw * (1.0 + sc) + sh


def _norm_kernel(x_ref, nw_ref, sc_ref, sh_ref, o_ref):
    o_ref[...] = _rms_mod(x_ref[...], nw_ref[...], sc_ref[0], sh_ref[0]).astype(o_ref.dtype)


def _norm_mod(x, nw, mod, k_sc, k_sh, seq, tm):
    m, d = x.shape
    nb = m // seq
    return pl.pallas_call(
        _norm_kernel,
        grid=(m // tm,),
        in_specs=[pl.BlockSpec((tm, d), lambda i: (i, 0)),
                  pl.BlockSpec((1, d), lambda i: (0, 0)),
                  pl.BlockSpec((1, 1, d), lambda i: (k_sc * nb + (i * tm) // seq, 0, 0)),
                  pl.BlockSpec((1, 1, d), lambda i: (k_sh * nb + (i * tm) // seq, 0, 0))],
        out_specs=pl.BlockSpec((tm, d), lambda i: (i, 0)),
        out_shape=jax.ShapeDtypeStruct((m, d), BF16),
        compiler_params=_cparams(("parallel",)),
        name="norm_mod",
    )(x, nw.reshape(1, d), mod, mod)


def _final_norm_kernel(x_ref, nw_ref, o_ref):
    x = x_ref[...]
    ms = jnp.mean(x * x, axis=-1, keepdims=True)
    o_ref[...] = x * lax.rsqrt(ms + NORM_EPS) * nw_ref[...]


def _final_norm(x, nw, tm):
    m, d = x.shape
    return pl.pallas_call(
        _final_norm_kernel,
        grid=(m // tm,),
        in_specs=[pl.BlockSpec((tm, d), lambda i: (i, 0)),
                  pl.BlockSpec((1, d), lambda i: (0, 0))],
        out_specs=pl.BlockSpec((tm, d), lambda i: (i, 0)),
        out_shape=jax.ShapeDtypeStruct((m, d), F32),
        compiler_params=_cparams(("parallel",)),
        name="final_norm",
    )(x, nw.reshape(1, d))


def _mm_kernel(a_ref, wt_ref, o_ref):
    o_ref[...] = _dot_nt(a_ref[...], wt_ref[...]).astype(o_ref.dtype)


def _matmul(a, wt, out_dtype, tm, tn, name):
    m, k = a.shape
    n = wt.shape[0]
    return pl.pallas_call(
        _mm_kernel,
        grid=(n // tn, m // tm),
        in_specs=[pl.BlockSpec((tm, k), lambda j, i: (i, 0)),
                  pl.BlockSpec((tn, k), lambda j, i: (j, 0))],
        out_specs=pl.BlockSpec((tm, tn), lambda j, i: (i, j)),
        out_shape=jax.ShapeDtypeStruct((m, n), out_dtype),
        compiler_params=_cparams(("parallel", "parallel")),
        name=name,
    )(a, wt)


def _mm_t_kernel(a_ref, wt_ref, o_ref):
    o_ref[...] = _dot_nt(wt_ref[...], a_ref[...]).astype(o_ref.dtype)


def _matmul_t(a, wt, out_dtype, tm, name):
    m, k = a.shape
    n = wt.shape[0]
    return pl.pallas_call(
        _mm_t_kernel,
        grid=(m // tm,),
        in_specs=[pl.BlockSpec((tm, k), lambda i: (i, 0)),
                  pl.BlockSpec((n, k), lambda i: (0, 0))],
        out_specs=pl.BlockSpec((n, tm), lambda i: (0, i)),
        out_shape=jax.ShapeDtypeStruct((n, m), out_dtype),
        compiler_params=_cparams(("parallel",)),
        name=name,
    )(a, wt)


def _gelu_tanh(x):
    c = math.sqrt(2.0 / math.pi)
    return 0.5 * x * (1.0 + jnp.tanh(c * (x + 0.044715 * (x * x * x))))


def _gmlp_kernel(uv_ref, lnw_ref, lnb_ref, ws_ref, bs_ref, o_ref, *, tm):
    g = _gelu_tanh(uv_ref[...])
    u = g[:, :GMLP_WIDTH]
    v = g[:, GMLP_WIDTH:]
    mu = jnp.mean(v, axis=-1, keepdims=True)
    vc = v - mu
    var = jnp.mean(vc * vc, axis=-1, keepdims=True)
    vn = (vc * lax.rsqrt(var + NORM_EPS) * lnw_ref[...] + lnb_ref[...]).astype(BF16)
    t = GMLP_CHUNK
    row = lax.broadcasted_iota(jnp.int32, (t, t), 0)
    col = lax.broadcasted_iota(jnp.int32, (t, t), 1)
    causal = row >= col
    ws = [jnp.where(causal, ws_ref[i], 0.0).astype(BF16) for i in range(GMLP_GROUPS)]
    first_head = lax.broadcasted_iota(jnp.int32, (t, LANES), 1) < HEAD_DIM
    for c in range(tm // t):
        rows = slice(c * t, (c + 1) * t)
        for p in range(GMLP_WIDTH // LANES):
            cols = slice(p * LANES, (p + 1) * LANES)
            vp = vn[rows, cols]
            mixed = jnp.where(first_head, _dot(ws[2 * p], vp), _dot(ws[2 * p + 1], vp))
            o_ref[rows, cols] = (u[rows, cols] * (mixed + bs_ref[:, cols])).astype(o_ref.dtype)


def _gmlp(uv, ln_w, ln_b, ws, bs, tm):
    m = uv.shape[0]
    w = GMLP_WIDTH
    bs_full = jnp.repeat(bs.T, HEAD_DIM, axis=1)
    return pl.pallas_call(
        functools.partial(_gmlp_kernel, tm=tm),
        grid=(m // tm,),
        in_specs=[pl.BlockSpec((tm, 2 * w), lambda i: (i, 0)),
                  pl.BlockSpec((1, w), lambda i: (0, 0)),
                  pl.BlockSpec((1, w), lambda i: (0, 0)),
                  pl.BlockSpec((GMLP_GROUPS, GMLP_CHUNK, GMLP_CHUNK), lambda i: (0, 0, 0)),
                  pl.BlockSpec((GMLP_CHUNK, w), lambda i: (0, 0))],
        out_specs=pl.BlockSpec((tm, w), lambda i: (i, 0)),
        out_shape=jax.ShapeDtypeStruct((m, w), BF16),
        compiler_params=_cparams(("parallel",)),
        name="gmlp",
    )(uv, ln_w.reshape(1, w), ln_b.reshape(1, w), ws, bs_full)


def _fcum_kernel(f_ref, b_ref, o_ref, *, seq):
    t = LANES
    r = lax.broadcasted_iota(jnp.int32, (LANES, LANES), 0)
    c = lax.broadcasted_iota(jnp.int32, (LANES, LANES), 1)
    spread = [jnp.where((c == 3 * r + part) & (r < FOX_HEADS), 1.0, 0.0).astype(BF16) for part in range(3)]

    def body(i, carry):
        r0 = pl.multiple_of(i * t, t)
        z = f_ref[pl.ds(r0, t), :] + b_ref[...]
        logf = jnp.minimum(z, 0.0) - jnp.log(1.0 + jnp.exp(-jnp.abs(z)))
        cs = _tri_cumsum(logf) + carry
        hi, mid, lo = _split3(-cs)
        o_ref[pl.ds(r0, t), :] = (_dot(hi, spread[0]) + _dot(mid, spread[1]) + _dot(lo, spread[2])).astype(o_ref.dtype)
        return cs[t - 1:t, :]

    lax.fori_loop(0, seq // t, body, jnp.zeros((1, LANES), F32))


def _forget_cumsum(proj, col_block, f_bias, seq):
    m = proj.shape[0]
    fb = jnp.pad(f_bias, (0, LANES - FOX_HEADS)).reshape(1, LANES)
    return pl.pallas_call(
        functools.partial(_fcum_kernel, seq=seq),
        grid=(m // seq,),
        in_specs=[pl.BlockSpec((seq, LANES), lambda b: (b, col_block)),
                  pl.BlockSpec((1, LANES), lambda b: (0, 0))],
        out_specs=pl.BlockSpec((seq, LANES), lambda b: (b, 0)),
        out_shape=jax.ShapeDtypeStruct((m, LANES), BF16),
        compiler_params=_cparams(("parallel",)),
        name="fox_cumsum",
    )(proj, fb)


def _attn_kernel(*refs, tq, tk, nblk, mode):
    if mode == "fox":
        q_ref, k_ref, ka_ref, vt_ref, o_ref, vp_scr, acc_scr, st_scr, p_scr, qc_scr = refs
    else:
        q_ref, ns_ref, k_ref, vt_ref, o_ref, vp_scr, acc_scr, st_scr, p_scr, qc_scr = refs
    p = pl.program_id(1)
    qi = pl.program_id(2)
    half = LANES // 2
    n_sub = tq // tk

    @pl.when(qi == 0)
    def _():
        chan = lax.broadcasted_iota(jnp.int32, (LANES, tk), 0)
        for jb in range(nblk):
            vt = vt_ref[:, jb * tk:(jb + 1) * tk]
            ones = jnp.ones_like(vt)
            vp_scr[0, jb] = jnp.where(chan < half, vt, ones)
            vp_scr[1, jb] = jnp.where(chan >= half, vt, ones)

    lane_q = lax.broadcasted_iota(jnp.int32, (tq, LANES), 1)
    lane_k = lax.broadcasted_iota(jnp.int32, (tk, LANES), 1)
    q = q_ref[...]
    if mode == "fox":
        q = q * (HEAD_DIM ** -0.5)
    slots = LANES // MOBA_HEADS
    for h in range(2):
        head_lanes = (lane_q < half) if h == 0 else (lane_q >= half)
        qh = jnp.where(head_lanes, q, jnp.zeros_like(q))
        if mode == "fox":
            first = 3 * (2 * p + h)
            aux = jnp.where((lane_q >= first) & (lane_q < first + 3), 1.0, 0.0).astype(BF16)
        else:
            aux = ns_ref[...]
        qc_scr[h] = jnp.concatenate([qh, aux], axis=1)

    def scores(j):
        r0 = pl.multiple_of(j * tk, tk)
        kb = k_ref[pl.ds(r0, tk), :]
        out = []
        for h in range(2):
            if mode == "fox":
                ka = ka_ref[pl.ds(r0, tk), :]
            else:
                ka = jnp.where(lane_k == (2 * p + h) * slots + j, NEG_BIG, 0.0).astype(BF16)
            out.append(_dot_nt(jnp.concatenate([kb, ka], axis=1), qc_scr[h]))
        return out

    def advance(j, ms, mask, prefetch):
        nxt = scores(j + 1) if prefetch else None
        prev = jnp.maximum(j - 1, 0)
        pv = [_dot(vp_scr[h, prev], p_scr[h]) for h in range(2)]
        ms2 = []
        for h in range(2):
            st = st_scr[h]
            if mask is not None:
                st = jnp.where(mask, st, NEG_BIG)
            m_new = jnp.maximum(ms[h], jnp.max(st, axis=0, keepdims=True))
            alpha = jnp.exp(ms[h] - m_new)
            pt = jnp.exp(st - m_new).astype(BF16)
            acc_scr[h] = alpha * (acc_scr[h] + pv[h])
            p_scr[h] = pt
            ms2.append(m_new)
        if prefetch:
            for h in range(2):
                st_scr[h] = nxt[h]
        return tuple(ms2)

    acc_scr[...] = jnp.zeros(acc_scr.shape, F32)
    p_scr[...] = jnp.zeros(p_scr.shape, BF16)
    first_scores = scores(0)
    for h in range(2):
        st_scr[h] = first_scores[h]
    m0 = jnp.full((1, tq), -jnp.inf, F32)
    first_diag = qi * n_sub

    def body(i, ms):
        for s in range(n_sub):
            ms = advance(i * n_sub + s, ms, None, True)
        return ms

    ms = lax.fori_loop(0, qi, body, (m0, m0))
    key_i = lax.broadcasted_iota(jnp.int32, (tk, tq), 0)
    qry_i = lax.broadcasted_iota(jnp.int32, (tk, tq), 1)
    for s in range(n_sub):
        ms = advance(first_diag + s, ms, (key_i + s * tk) <= qry_i, s + 1 < n_sub)
    last = first_diag + n_sub - 1
    outs = []
    for h in range(2):
        a = acc_scr[h] + _dot(vp_scr[h, last], p_scr[h])
        denom = a[(1 - h) * half:(1 - h) * half + 1, :]
        outs.append(a[h * half:(h + 1) * half, :] / denom)
    o_ref[...] = jnp.concatenate(outs, axis=0).T.astype(o_ref.dtype)


def _block_attention(mode, q_arr, q_col, aux_arr, k_arr, k_col, vt_arr, vt_row, seq, tq, tk):
    m = q_arr.shape[0]
    nb = m // seq
    nq = seq // tq
    pairs = FOX_HEADS // 2
    q_spec = pl.BlockSpec((tq, LANES), lambda b, p, i: (b * nq + i, q_col + p))
    k_spec = pl.BlockSpec((seq, LANES), lambda b, p, i: (b, k_col + p))
    vt_spec = pl.BlockSpec((LANES, seq), lambda b, p, i: (vt_row + p, b))
    if mode == "fox":
        in_specs = [q_spec, k_spec, pl.BlockSpec((seq, LANES), lambda b, p, i: (b, 0)), vt_spec]
        args = (q_arr, k_arr, aux_arr, vt_arr)
    else:
        in_specs = [q_spec, pl.BlockSpec((tq, LANES), lambda b, p, i: (b * nq + i, 0)), k_spec, vt_spec]
        args = (q_arr, aux_arr, k_arr, vt_arr)
    return pl.pallas_call(
        functools.partial(_attn_kernel, tq=tq, tk=tk, nblk=seq // tk, mode=mode),
        grid=(nb, pairs, nq),
        in_specs=in_specs,
        out_specs=pl.BlockSpec((tq, LANES), lambda b, p, i: (b * nq + i, p)),
        out_shape=jax.ShapeDtypeStruct((m, pairs * LANES), BF16),
        scratch_shapes=[pltpu.VMEM((2, seq // tk, LANES, tk), BF16), pltpu.VMEM((2, LANES, tq), F32),
                        pltpu.VMEM((2, tk, tq), F32), pltpu.VMEM((2, tk, tq), BF16),
                        pltpu.VMEM((2, tq, 2 * LANES), BF16)],
        compiler_params=_cparams(("arbitrary", "arbitrary", "arbitrary")),
        name=mode + "_attn",
    )(*args)


def _rope_tables(seq):
    half = ROPE_DIM // 2
    inv_freq = ROPE_THETA ** (-jnp.arange(half, dtype=F32) / half)
    ang = jnp.arange(seq, dtype=F32)[:, None] * inv_freq[None, :]
    cos, sin = jnp.cos(ang), jnp.sin(ang)
    ones = jnp.ones((seq, HEAD_DIM - ROPE_DIM), F32)
    cos_h = jnp.concatenate([cos, cos, ones], axis=1)
    sin_h = jnp.concatenate([-sin, sin, 0.0 * ones], axis=1)
    return jnp.tile(cos_h, (1, 2)), jnp.tile(sin_h, (1, 2))


def _rotary(x, cos_t, sin_t):
    half = ROPE_DIM // 2
    lane = lax.broadcasted_iota(jnp.int32, (x.shape[0], LANES), 1)
    first_half = (lane & (HEAD_DIM - 1)) < half
    outs = []
    for j in range(x.shape[1] // LANES):
        xc = x[:, j * LANES:(j + 1) * LANES]
        up = pltpu.roll(xc, LANES - half, axis=1)
        down = pltpu.roll(xc, half, axis=1)
        outs.append(xc * cos_t + jnp.where(first_half, up, down) * sin_t)
    return jnp.concatenate(outs, axis=1)


def _moba_k_kernel(k_ref, cos_ref, sin_ref, ko_ref, km_ref):
    kr = _rotary(k_ref[...], cos_ref[...], sin_ref[...])
    ko_ref[...] = kr.astype(ko_ref.dtype)
    km_ref[0] = jnp.mean(kr, axis=0, keepdims=True)


def _moba_k_prep(proj, k_col, cos_t, sin_t, seq):
    m = proj.shape[0]
    t = MOBA_BLOCK
    nblk = seq // t
    return pl.pallas_call(
        _moba_k_kernel,
        grid=(m // t,),
        in_specs=[pl.BlockSpec((t, MOBA_WIDTH), lambda i: (i, k_col)),
                  pl.BlockSpec((t, LANES), lambda i: (i % nblk, 0)),
                  pl.BlockSpec((t, LANES), lambda i: (i % nblk, 0))],
        out_specs=[pl.BlockSpec((t, MOBA_WIDTH), lambda i: (i, 0)),
                   pl.BlockSpec((1, 1, MOBA_WIDTH), lambda i: (i, 0, 0))],
        out_shape=[jax.ShapeDtypeStruct((m, MOBA_WIDTH), BF16),
                   jax.ShapeDtypeStruct((m // t, 1, MOBA_WIDTH), F32)],
        compiler_params=_cparams(("parallel",)),
        name="moba_k_prep",
    )(proj, cos_t, sin_t)


def _moba_q_kernel(q_ref, cos_ref, sin_ref, km_ref, qo_ref, ns_ref, *, nblk):
    own = pl.program_id(0) % nblk
    qr = _rotary(q_ref[...], cos_ref[...], sin_ref[...])
    qo_ref[...] = (qr * (HEAD_DIM ** -0.5)).astype(qo_ref.dtype)
    gate = _dot_f32(qr, km_ref[0])
    slots = LANES // MOBA_HEADS
    lane = lax.broadcasted_iota(jnp.int32, gate.shape, 1)
    n = lane & (slots - 1)
    cnt = jnp.zeros(gate.shape, jnp.int32)
    for r in range(1, slots):
        wrapped = (n + r) >= slots
        other = jnp.where(wrapped, pltpu.roll(gate, slots - r, axis=1), pltpu.roll(gate, LANES - r, axis=1))
        other_n = jnp.where(wrapped, n + (r - slots), n + r)
        beats = (other > gate) | ((other == gate) & wrapped)
        cnt = cnt + jnp.where((other_n < own) & beats, 1, 0)
    selected = (n == own) | ((n < own) & (cnt < MOBA_TOPK))
    ns_ref[...] = jnp.where(selected, 0.0, 1.0).astype(ns_ref.dtype)


def _moba_q_prep(proj, q_col, cos_t, sin_t, km_mat, seq):
    m = proj.shape[0]
    t = MOBA_BLOCK
    nblk = seq // t
    return pl.pallas_call(
        functools.partial(_moba_q_kernel, nblk=nblk),
        grid=(m // t,),
        in_specs=[pl.BlockSpec((t, MOBA_WIDTH), lambda i: (i, q_col)),
                  pl.BlockSpec((t, LANES), lambda i: (i % nblk, 0)),
                  pl.BlockSpec((t, LANES), lambda i: (i % nblk, 0)),
                  pl.BlockSpec((1, MOBA_WIDTH, LANES), lambda i: (i // nblk, 0, 0))],
        out_specs=[pl.BlockSpec((t, MOBA_WIDTH), lambda i: (i, 0)),
                   pl.BlockSpec((t, LANES), lambda i: (i, 0))],
        out_shape=[jax.ShapeDtypeStruct((m, MOBA_WIDTH), BF16),
                   jax.ShapeDtypeStruct((m, LANES), BF16)],
        compiler_params=_cparams(("parallel",)),
        name="moba_q_prep",
    )(proj, cos_t, sin_t, km_mat)


def _pair_cols(x, h0, h1, shape):
    lane = lax.broadcasted_iota(jnp.int32, shape, 1)
    a = jnp.broadcast_to(x[:, h0:h0 + 1], shape)
    b = jnp.broadcast_to(x[:, h1:h1 + 1], shape)
    return jnp.where(lane < SSM_HEAD_DIM, a, b)


def _ssd_kernel(in_ref, cw_ref, cb_ref, dtb_ref, alog_ref, dsk_ref, nw_ref, o_ref, xpad_scr, st_scr):
    t = SSM_CHUNK
    c = pl.program_id(1)
    halo = SUBLANES

    @pl.when(c == 0)
    def _():
        xpad_scr[0:halo, :] = jnp.zeros((halo, SSM_CONV_DIM), F32)
        st_scr[...] = jnp.zeros(st_scr.shape, F32)

    xpad_scr[halo:halo + t, :] = in_ref[:, 0:SSM_CONV_DIM]
    conv = cb_ref[...] + cw_ref[0:1, :] * xpad_scr[pl.ds(halo - SSM_CONV + 1, t), :]
    for k in range(1, SSM_CONV):
        conv = conv + cw_ref[k:k + 1, :] * xpad_scr[pl.ds(halo - SSM_CONV + 1 + k, t), :]
    xpad_scr[0:halo, :] = xpad_scr[t:t + halo, :]
    xbc = conv * _sigmoid(conv)
    z = in_ref[:, SSM_CONV_DIM:SSM_CONV_DIM + SSM_WIDTH]
    dt = _softplus(in_ref[:, SSM_CONV_DIM + SSM_WIDTH:] + dtb_ref[...])
    a = -jnp.exp(alog_ref[...])
    acum = _tri_cumsum(dt * a)
    acum_t = acum.T
    tot = acum[t - 1:t, :]
    decay_end = jnp.exp(tot - acum)
    decay_in = jnp.exp(acum)
    exp_tot = jnp.exp(tot)
    row = lax.broadcasted_iota(jnp.int32, (t, t), 0)
    col = lax.broadcasted_iota(jnp.int32, (t, t), 1)
    causal = row >= col
    lane = lax.broadcasted_iota(jnp.int32, (t, LANES), 1)
    lane1 = lax.broadcasted_iota(jnp.int32, (1, LANES), 1)
    pair_shape = (t, LANES)
    rep = SSM_HEADS // SSM_GROUPS
    b_off = SSM_WIDTH
    c_off = SSM_WIDTH + SSM_GROUPS * SSM_STATE
    ys = []
    for g in range(SSM_GROUPS):
        bm = xbc[:, b_off + g * SSM_STATE:b_off + (g + 1) * SSM_STATE]
        cm = xbc[:, c_off + g * SSM_STATE:c_off + (g + 1) * SSM_STATE].astype(BF16)
        cb = _dot_nt(cm, bm.astype(BF16))
        bm_t = bm.T.astype(BF16)
        for pr in range(rep // 2):
            j = g * (rep // 2) + pr
            h0, h1 = 2 * j, 2 * j + 1
            xp = xbc[:, j * LANES:(j + 1) * LANES]
            xdt = xp * _pair_cols(dt, h0, h1, pair_shape)
            xdt_b = xdt.astype(BF16)
            yd = []
            for h in (h0, h1):
                diff = acum[:, h:h + 1] - acum_t[h:h + 1, :]
                w = cb * jnp.exp(jnp.where(causal, diff, NEG_BIG))
                yd.append(_dot(w.astype(BF16), xdt_b))
            y_diag = jnp.where(lane < SSM_HEAD_DIM, yd[0], yd[1])
            prev = st_scr[j]
            y_off = _dot(cm, prev.astype(BF16)) * _pair_cols(decay_in, h0, h1, pair_shape)
            xw = (xdt * _pair_cols(decay_end, h0, h1, pair_shape)).astype(BF16)
            scale = jnp.where(lane1 < SSM_HEAD_DIM, exp_tot[:, h0:h0 + 1], exp_tot[:, h1:h1 + 1])
            st_scr[j] = prev * scale + _dot(bm_t, xw)
            ys.append(y_diag + y_off + dsk_ref[:, j * LANES:(j + 1) * LANES] * xp)
    y = jnp.concatenate(ys, axis=1)
    y = y * (z * _sigmoid(z))
    gw = SSM_WIDTH // SSM_GROUPS
    outs = []
    for g in range(SSM_GROUPS):
        seg = y[:, g * gw:(g + 1) * gw]
        ms = jnp.mean(seg * seg, axis=-1, keepdims=True)
        outs.append(seg * lax.rsqrt(ms + NORM_EPS) * nw_ref[:, g * gw:(g + 1) * gw])
    o_ref[...] = jnp.concatenate(outs, axis=1).astype(o_ref.dtype)


def _ssd(proj, conv_w, conv_b, dt_bias, a_log, d_skip, norm_w, seq):
    m, wtot = proj.shape
    t = SSM_CHUNK
    nc = seq // t
    pad = LANES - SSM_HEADS
    dtb = jnp.pad(dt_bias, (0, pad)).reshape(1, LANES)
    alog = jnp.pad(a_log, (0, pad)).reshape(1, LANES)
    dsk = jnp.repeat(d_skip, SSM_HEAD_DIM).reshape(1, SSM_WIDTH)
    const = lambda b, c: (0, 0)
    return pl.pallas_call(
        _ssd_kernel,
        grid=(m // seq, nc),
        in_specs=[pl.BlockSpec((t, wtot), lambda b, c: (b * nc + c, 0)),
                  pl.BlockSpec((SSM_CONV, SSM_CONV_DIM), const),
                  pl.BlockSpec((1, SSM_CONV_DIM), const),
                  pl.BlockSpec((1, LANES), const),
                  pl.BlockSpec((1, LANES), const),
                  pl.BlockSpec((1, SSM_WIDTH), const),
                  pl.BlockSpec((1, SSM_WIDTH), const)],
        out_specs=pl.BlockSpec((t, SSM_WIDTH), lambda b, c: (b * nc + c, 0)),
        out_shape=jax.ShapeDtypeStruct((m, SSM_WIDTH), BF16),
        scratch_shapes=[pltpu.VMEM((t + SUBLANES, SSM_CONV_DIM), F32),
                        pltpu.VMEM((SSM_HEADS // 2, SSM_STATE, LANES), F32)],
        compiler_params=_cparams(("parallel", "arbitrary")),
        name="ssd",
    )(proj, conv_w, conv_b.reshape(1, SSM_CONV_DIM), dtb, alog, dsk, norm_w.reshape(1, SSM_WIDTH))


def _merge_kernel(ya_ref, yb_ref, yc_ref, yd_ref, g_ref, x_ref, g1_ref,
                  wa_ref, wb_ref, wc_ref, wd_ref, wo_ref, o_ref):
    d = x_ref.shape[1]
    gate = lambda i: _sigmoid(g_ref[:, i * d:(i + 1) * d].astype(F32))
    merged = gate(0) * _dot(ya_ref[...], wa_ref[...])
    merged = merged + gate(1) * _dot(yb_ref[...], wb_ref[...])
    merged = merged + gate(2) * _dot(yc_ref[...], wc_ref[...])
    merged = merged + gate(3) * _dot(yd_ref[...], wd_ref[...])
    o_ref[...] = x_ref[...] + g1_ref[0] * _dot(merged.astype(BF16), wo_ref[...])


def _merge(ya, yb, yc, yd, gates, x, mod, k_gate, wa, wb, wc, wd, wo, seq, tm):
    m, d = x.shape
    nb = m // seq
    rows = lambda i: (i, 0)
    const = lambda i: (0, 0)
    return pl.pallas_call(
        _merge_kernel,
        grid=(m // tm,),
        in_specs=[pl.BlockSpec((tm, ya.shape[1]), rows),
                  pl.BlockSpec((tm, yb.shape[1]), rows),
                  pl.BlockSpec((tm, yc.shape[1]), rows),
                  pl.BlockSpec((tm, yd.shape[1]), rows),
                  pl.BlockSpec((tm, N_BRANCH * d), rows),
                  pl.BlockSpec((tm, d), rows),
                  pl.BlockSpec((1, 1, d), lambda i: (k_gate * nb + (i * tm) // seq, 0, 0)),
                  pl.BlockSpec(wa.shape, const),
                  pl.BlockSpec(wb.shape, const),
                  pl.BlockSpec(wc.shape, const),
                  pl.BlockSpec(wd.shape, const),
                  pl.BlockSpec(wo.shape, const)],
        out_specs=pl.BlockSpec((tm, d), rows),
        out_shape=jax.ShapeDtypeStruct((m, d), F32),
        compiler_params=_cparams(("parallel",)),
        name="merge",
    )(ya, yb, yc, yd, gates, x, mod, wa, wb, wc, wd, wo)


def _ffn_kernel(x_ref, nw_ref, sc_ref, sh_ref, g_ref, w1_ref, w2_ref, o_ref, h_scr, acc_scr):
    j = pl.program_id(1)

    @pl.when(j == 0)
    def _():
        h_scr[...] = _rms_mod(x_ref[...], nw_ref[...], sc_ref[0], sh_ref[0]).astype(BF16)
        acc_scr[...] = jnp.zeros(acc_scr.shape, F32)

    a = jnp.maximum(_dot(h_scr[...], w1_ref[...]), 0.0)
    acc_scr[...] += _dot((a * a).astype(BF16), w2_ref[...])

    @pl.when(j == pl.num_programs(1) - 1)
    def _():
        o_ref[...] = x_ref[...] + g_ref[0] * acc_scr[...]


def _ffn(x, nw, mod, k_sc, k_sh, k_gate, w1, w2, seq, tm, tf):
    m, d = x.shape
    nb = m // seq
    dff = w1.shape[1]
    modrow = lambda k: (lambda i, j: (k * nb + (i * tm) // seq, 0, 0))
    return pl.pallas_call(
        _ffn_kernel,
        grid=(m // tm, dff // tf),
        in_specs=[pl.BlockSpec((tm, d), lambda i, j: (i, 0)),
                  pl.BlockSpec((1, d), lambda i, j: (0, 0)),
                  pl.BlockSpec((1, 1, d), modrow(k_sc)),
                  pl.BlockSpec((1, 1, d), modrow(k_sh)),
                  pl.BlockSpec((1, 1, d), modrow(k_gate)),
                  pl.BlockSpec((d, tf), lambda i, j: (0, j)),
                  pl.BlockSpec((tf, d), lambda i, j: (j, 0))],
        out_specs=pl.BlockSpec((tm, d), lambda i, j: (i, 0)),
        out_shape=jax.ShapeDtypeStruct((m, d), F32),
        scratch_shapes=[pltpu.VMEM((tm, d), BF16), pltpu.VMEM((tm, d), F32)],
        compiler_params=_cparams(("parallel", "arbitrary")),
        name="ffn",
    )(x, nw.reshape(1, d), mod, mod, mod, w1, w2)


def _pad_rows(w, rows):
    return jnp.pad(w, ((0, rows - w.shape[0]), (0, 0)))


def _split_w_in(w_in, l):
    wt = jnp.transpose(w_in, (2, 0, 1))[:, l, :]
    o = 0
    uv = wt[o:o + 2 * GMLP_WIDTH]; o += 2 * GMLP_WIDTH
    fox_qk = wt[o:o + 2 * FOX_WIDTH]; o += 2 * FOX_WIDTH
    fox_v = wt[o:o + FOX_WIDTH]; o += FOX_WIDTH
    fox_f = wt[o:o + FOX_HEADS]; o += FOX_HEADS
    moba_qk = wt[o:o + 2 * MOBA_WIDTH]; o += 2 * MOBA_WIDTH
    moba_v = wt[o:o + MOBA_WIDTH]; o += MOBA_WIDTH
    z = wt[o:o + SSM_WIDTH]; o += SSM_WIDTH
    xbc = wt[o:o + SSM_CONV_DIM]; o += SSM_CONV_DIM
    dt = wt[o:o + SSM_HEADS]; o += SSM_HEADS
    gates = wt[o:]
    w_v = jnp.concatenate([fox_v, moba_v], axis=0)
    w_mqkf = jnp.concatenate([moba_qk, _pad_rows(fox_f, LANES)], axis=0)
    w_ssd = jnp.concatenate([xbc, z, _pad_rows(dt, LANES)], axis=0)
    return [g.astype(BF16) for g in (uv, fox_qk, w_v, w_mqkf, w_ssd, gates)]


def _tile_rows(seq, want):
    return min(want, seq)


def kernel(x, c, ada_w, ada_b, norm_mix_w, w_in, gmlp_ln_w, gmlp_ln_b, gmlp_ws, gmlp_bs, fox_f_bias, ssm_conv_w, ssm_conv_b, ssm_dt_bias, ssm_a_log, ssm_d, ssm_norm_w, w_branch_a, w_branch_b, w_branch_c, w_branch_d, w_out, norm_mlp_w, mlp_w1, mlp_w2, final_norm_w):
    nb, seq, d = x.shape
    m = nb * seq
    depth = ada_w.shape[0]
    tm = _tile_rows(seq, 512)
    tq_attn = _tile_rows(seq, ATTN_Q_TILE)
    pairs = FOX_HEADS // 2
    nblk = seq // MOBA_BLOCK
    slots = LANES // MOBA_HEADS
    assert seq % MOBA_BLOCK == 0 and nblk <= slots and seq % SSM_CHUNK == 0

    mod_all = _modulation(c, ada_w, ada_b)[:, :nb]
    mod_all = mod_all.reshape(depth, nb, 6, d).transpose(0, 2, 1, 3).reshape(depth, 6 * nb, 1, d)
    cos_t, sin_t = _rope_tables(seq)
    head_eye = jnp.eye(MOBA_HEADS, dtype=F32)

    xf = x.reshape(m, d)
    for l in range(depth):
        mod = mod_all[l]
        w_uv, w_fqk, w_v, w_mqkf, w_ssd, w_gate = _split_w_in(w_in, l)
        h = _norm_mod(xf, norm_mix_w[l], mod, 1, 0, seq, tm)
        p_uv = _matmul(h, w_uv, F32, tm, w_uv.shape[0], "proj_uv")
        p_fqk = _matmul(h, w_fqk, BF16, tm, w_fqk.shape[0], "proj_fox_qk")
        p_vt = _matmul_t(h, w_v, BF16, tm, "proj_vt")
        p_mqkf = _matmul(h, w_mqkf, F32, tm, w_mqkf.shape[0], "proj_mqkf")
        p_ssd = _matmul(h, w_ssd, F32, tm, w_ssd.shape[0], "proj_ssd")
        p_gate = _matmul(h, w_gate, BF16, tm, 2 * d, "proj_gate")

        y_a = _gmlp(p_uv, gmlp_ln_w[l], gmlp_ln_b[l], gmlp_ws[l], gmlp_bs[l], tm)

        k_aux = _forget_cumsum(p_mqkf, 2 * MOBA_WIDTH // LANES, fox_f_bias[l], seq)
        y_b = _block_attention("fox", p_fqk, 0, k_aux, p_fqk, pairs, p_vt, 0, seq, tq_attn, MOBA_BLOCK)

        k_rot, k_mean = _moba_k_prep(p_mqkf, 1, cos_t, sin_t, seq)
        km = k_mean.reshape(nb, nblk, MOBA_HEADS, HEAD_DIM).transpose(0, 2, 3, 1)
        km = jnp.pad(km, ((0, 0), (0, 0), (0, 0), (0, slots - nblk)))
        km_mat = jnp.einsum('bhdn,hk->bhdkn', km, head_eye).reshape(nb, MOBA_WIDTH, LANES)
        q_rot, not_sel = _moba_q_prep(p_mqkf, 0, cos_t, sin_t, km_mat, seq)
        y_c = _block_attention("moba", q_rot, 0, not_sel, k_rot, 0, p_vt, pairs, seq, tq_attn, MOBA_BLOCK)

        y_d = _ssd(p_ssd, ssm_conv_w[l], ssm_conv_b[l], ssm_dt_bias[l], ssm_a_log[l], ssm_d[l],
                   ssm_norm_w[l], seq)

        xf = _merge(y_a, y_b, y_c, y_d, p_gate, xf, mod, 2,
                    w_branch_a[l].astype(BF16), w_branch_b[l].astype(BF16), w_branch_c[l].astype(BF16),
                    w_branch_d[l].astype(BF16), w_out[l].astype(BF16), seq, tm)
        xf = _ffn(xf, norm_mlp_w[l], mod, 4, 3, 5, mlp_w1[l].astype(BF16), mlp_w2[l].astype(BF16),
                  seq, _tile_rows(seq, 1024), 1024)
    return _final_norm(xf, final_norm_w, tm).reshape(nb, seq, d)
```

```python
import functools
import math

import jax
import jax.numpy as jnp
from jax import lax
from jax.experimental import pallas as pl
from jax.experimental.pallas import tpu as pltpu

F32 = jnp.float32
BF16 = jnp.bfloat16

HEAD_DIM = 64
NORM_EPS = 1e-6
GMLP_GROUPS = 8
GMLP_WIDTH = GMLP_GROUPS * HEAD_DIM
GMLP_CHUNK = 128
FOX_HEADS = 8
FOX_WIDTH = FOX_HEADS * HEAD_DIM
MOBA_HEADS = 8
MOBA_WIDTH = MOBA_HEADS * HEAD_DIM
MOBA_BLOCK = 256
MOBA_TOPK = 3
ROPE_THETA = 500000.0
ROPE_DIM = HEAD_DIM // 4
SSM_HEADS = 12
SSM_HEAD_DIM = 64
SSM_WIDTH = SSM_HEADS * SSM_HEAD_DIM
SSM_GROUPS = 2
SSM_STATE = 128
SSM_CONV = 4
SSM_CHUNK = 128
SSM_CONV_DIM = SSM_WIDTH + 2 * SSM_GROUPS * SSM_STATE
N_BRANCH = 4

LANES = 128
SUBLANES = 8
NEG_BIG = -1e30
VMEM_LIMIT = 48 * 1024 * 1024
ATTN_Q_TILE = 512
DENOM_ROWS = 16
QK_SCALE = HEAD_DIM ** -0.5


def _cparams(sem):
    return pltpu.CompilerParams(dimension_semantics=sem, vmem_limit_bytes=VMEM_LIMIT)


def _sigmoid(x):
    return 1.0 / (1.0 + jnp.exp(-x))


def _softplus(x):
    return jnp.maximum(x, 0.0) + jnp.log(1.0 + jnp.exp(-jnp.abs(x)))


def _dot(a, b):
    return jnp.dot(a, b, preferred_element_type=F32)


def _dot_nt(a, b):
    return lax.dot_general(a, b, (((1,), (1,)), ((), ())), preferred_element_type=F32)


def _keep_high_half(x):
    bits = lax.bitcast_convert_type(x, jnp.uint32) & jnp.uint32(0xFFFF0000)
    return lax.bitcast_convert_type(bits, F32)


def _split3(x):
    hi = _keep_high_half(x)
    r = x - hi
    mid = _keep_high_half(r)
    lo = r - mid
    return hi.astype(BF16), mid.astype(BF16), lo.astype(BF16)


def _tri_cumsum(x):
    t = x.shape[0]
    row = lax.broadcasted_iota(jnp.int32, (t, t), 0)
    col = lax.broadcasted_iota(jnp.int32, (t, t), 1)
    tri = jnp.where(row >= col, 1.0, 0.0).astype(BF16)
    hi, mid, lo = _split3(x)
    return _dot(tri, hi) + _dot(tri, mid) + _dot(tri, lo)


def _six_pass(dot, a, b):
    ah, am, al = _split3(a)
    bh, bm, bl = _split3(b)
    small = dot(am, bm) + dot(ah, bl) + dot(al, bh)
    return dot(ah, bh) + (dot(ah, bm) + dot(am, bh)) + small


def _dot_f32(a, b):
    return _six_pass(_dot, a, b)


def _dot_f32_nt(a, b):
    return _six_pass(_dot_nt, a, b)


def _mod_kernel(c_ref, w_ref, b_ref, o_ref):
    c = c_ref[...]
    ca = c * _sigmoid(c)
    o_ref[0] = _dot_f32(ca, w_ref[0]) + b_ref[0]


def _modulation(c, ada_w, ada_b):
    nb, d = c.shape
    nl = ada_w.shape[0]
    bp = -(-nb // SUBLANES) * SUBLANES
    cp = jnp.pad(c, ((0, bp - nb), (0, 0)))
    return pl.pallas_call(
        _mod_kernel,
        grid=(nl, 6),
        in_specs=[pl.BlockSpec((bp, d), lambda l, k: (0, 0)),
                  pl.BlockSpec((1, d, d), lambda l, k: (l, 0, k)),
                  pl.BlockSpec((1, 1, d), lambda l, k: (l, 0, k))],
        out_specs=pl.BlockSpec((1, bp, d), lambda l, k: (l, 0, k)),
        out_shape=jax.ShapeDtypeStruct((nl, bp, 6 * d), F32),
        compiler_params=_cparams(("parallel", "parallel")),
        name="adaln_mod",
    )(cp, ada_w, ada_b.reshape(nl, 1, 6 * d))


def _rms_mod(x, nw, sc, sh):
    ms = jnp.mean(x * x, axis=-1, keepdims=True)
    xn = x * lax.rsqrt(ms + NORM_EPS)
    return xn * nw * (1.0 + sc) + sh


def _norm_kernel(x_ref, nw_ref, sc_ref, sh_ref, o_ref):
    o_ref[...] = _rms_mod(x_ref[...], nw_ref[...], sc_ref[0], sh_ref[0]).astype(o_ref.dtype)


def _norm_mod(x, nw, mod, k_sc, k_sh, seq, tm):
    m, d = x.shape
    nb = m // seq
    return pl.pallas_call(
        _norm_kernel,
        grid=(m // tm,),
        in_specs=[pl.BlockSpec((tm, d), lambda i: (i, 0)),
                  pl.BlockSpec((1, d), lambda i: (0, 0)),
                  pl.BlockSpec((1, 1, d), lambda i: (k_sc * nb + (i * tm) // seq, 0, 0)),
                  pl.BlockSpec((1, 1, d), lambda i: (k_sh * nb + (i * tm) // seq, 0, 0))],
        out_specs=pl.BlockSpec((tm, d), lambda i: (i, 0)),
        out_shape=jax.ShapeDtypeStruct((m, d), BF16),
        compiler_params=_cparams(("parallel",)),
        name="norm_mod",
    )(x, nw.reshape(1, d), mod, mod)


def _final_norm_kernel(x_ref, nw_ref, o_ref):
    x = x_ref[...]
    ms = jnp.mean(x * x, axis=-1, keepdims=True)
    o_ref[...] = x * lax.rsqrt(ms + NORM_EPS) * nw_ref[...]


def _final_norm(x, nw, tm):
    m, d = x.shape
    return pl.pallas_call(
        _final_norm_kernel,
        grid=(m // tm,),
        in_specs=[pl.BlockSpec((tm, d), lambda i: (i, 0)),
                  pl.BlockSpec((1, d), lambda i: (0, 0))],
        out_specs=pl.BlockSpec((tm, d), lambda i: (i, 0)),
        out_shape=jax.ShapeDtypeStruct((m, d), F32),
        compiler_params=_cparams(("parallel",)),
        name="final_norm",
    )(x, nw.reshape(1, d))


def _mm_kernel(a_ref, wt_ref, o_ref, *, lead_cols, lead_scale):
    out = _dot_nt(a_ref[...], wt_ref[...])
    if lead_cols:
        out = jnp.concatenate([out[:, :lead_cols] * lead_scale, out[:, lead_cols:]], axis=1)
    o_ref[...] = out.astype(o_ref.dtype)


def _matmul(a, wt, out_dtype, tm, tn, name, lead_cols=0, lead_scale=1.0):
    m, k = a.shape
    n = wt.shape[0]
    assert lead_cols == 0 or tn == n
    return pl.pallas_call(
        functools.partial(_mm_kernel, lead_cols=lead_cols, lead_scale=lead_scale),
        grid=(n // tn, m // tm),
        in_specs=[pl.BlockSpec((tm, k), lambda j, i: (i, 0)),
                  pl.BlockSpec((tn, k), lambda j, i: (j, 0))],
        out_specs=pl.BlockSpec((tm, tn), lambda j, i: (i, j)),
        out_shape=jax.ShapeDtypeStruct((m, n), out_dtype),
        compiler_params=_cparams(("parallel", "parallel")),
        name=name,
    )(a, wt)


def _mm_t_kernel(a_ref, wt_ref, o_ref):
    o_ref[...] = _dot_nt(wt_ref[...], a_ref[...]).astype(o_ref.dtype)


def _matmul_t(a, wt, out_dtype, tm, name):
    m, k = a.shape
    n = wt.shape[0]
    return pl.pallas_call(
        _mm_t_kernel,
        grid=(m // tm,),
        in_specs=[pl.BlockSpec((tm, k), lambda i: (i, 0)),
                  pl.BlockSpec((n, k), lambda i: (0, 0))],
        out_specs=pl.BlockSpec((n, tm), lambda i: (0, i)),
        out_shape=jax.ShapeDtypeStruct((n, m), out_dtype),
        compiler_params=_cparams(("parallel",)),
        name=name,
    )(a, wt)


def _gelu_tanh(x):
    c = math.sqrt(2.0 / math.pi)
    return 0.5 * x * (1.0 + jnp.tanh(c * (x + 0.044715 * (x * x * x))))


def _gmlp_kernel(uv_ref, lnw_ref, lnb_ref, ws_ref, bs_ref, o_ref, *, tm):
    g = _gelu_tanh(uv_ref[...])
    u = g[:, :GMLP_WIDTH]
    v = g[:, GMLP_WIDTH:]
    mu = jnp.mean(v, axis=-1, keepdims=True)
    vc = v - mu
    var = jnp.mean(vc * vc, axis=-1, keepdims=True)
    vn = (vc * lax.rsqrt(var + NORM_EPS) * lnw_ref[...] + lnb_ref[...]).astype(BF16)
    t = GMLP_CHUNK
    row = lax.broadcasted_iota(jnp.int32, (t, t), 0)
    col = lax.broadcasted_iota(jnp.int32, (t, t), 1)
    causal = row >= col
    ws = [jnp.where(causal, ws_ref[i], 0.0).astype(BF16) for i in range(GMLP_GROUPS)]
    first_head = lax.broadcasted_iota(jnp.int32, (t, LANES), 1) < HEAD_DIM
    for c in range(tm // t):
        rows = slice(c * t, (c + 1) * t)
        for p in range(GMLP_WIDTH // LANES):
            cols = slice(p * LANES, (p + 1) * LANES)
            vp = vn[rows, cols]
            mixed = jnp.where(first_head, _dot(ws[2 * p], vp), _dot(ws[2 * p + 1], vp))
            o_ref[rows, cols] = (u[rows, cols] * (mixed + bs_ref[:, cols])).astype(o_ref.dtype)


def _gmlp(uv, ln_w, ln_b, ws, bs, tm):
    m = uv.shape[0]
    w = GMLP_WIDTH
    bs_full = jnp.repeat(bs.T, HEAD_DIM, axis=1)
    return pl.pallas_call(
        functools.partial(_gmlp_kernel, tm=tm),
        grid=(m // tm,),
        in_specs=[pl.BlockSpec((tm, 2 * w), lambda i: (i, 0)),
                  pl.BlockSpec((1, w), lambda i: (0, 0)),
                  pl.BlockSpec((1, w), lambda i: (0, 0)),
                  pl.BlockSpec((GMLP_GROUPS, GMLP_CHUNK, GMLP_CHUNK), lambda i: (0, 0, 0)),
                  pl.BlockSpec((GMLP_CHUNK, w), lambda i: (0, 0))],
        out_specs=pl.BlockSpec((tm, w), lambda i: (i, 0)),
        out_shape=jax.ShapeDtypeStruct((m, w), BF16),
        compiler_params=_cparams(("parallel",)),
        name="gmlp",
    )(uv, ln_w.reshape(1, w), ln_b.reshape(1, w), ws, bs_full)


def _fcum_kernel(f_ref, b_ref, o_ref, *, seq):
    t = LANES
    r = lax.broadcasted_iota(jnp.int32, (LANES, LANES), 0)
    c = lax.broadcasted_iota(jnp.int32, (LANES, LANES), 1)
    spread = [jnp.where((c == 3 * r + part) & (r < FOX_HEADS), 1.0, 0.0).astype(BF16) for part in range(3)]

    def body(i, carry):
        r0 = pl.multiple_of(i * t, t)
        z = f_ref[pl.ds(r0, t), :] + b_ref[...]
        logf = jnp.minimum(z, 0.0) - jnp.log(1.0 + jnp.exp(-jnp.abs(z)))
        cs = _tri_cumsum(logf) + carry
        hi, mid, lo = _split3(-cs)
        o_ref[pl.ds(r0, t), :] = (_dot(hi, spread[0]) + _dot(mid, spread[1]) + _dot(lo, spread[2])).astype(o_ref.dtype)
        return cs[t - 1:t, :]

    lax.fori_loop(0, seq // t, body, jnp.zeros((1, LANES), F32))


def _forget_cumsum(proj, col_block, f_bias, seq):
    m = proj.shape[0]
    fb = jnp.pad(f_bias, (0, LANES - FOX_HEADS)).reshape(1, LANES)
    return pl.pallas_call(
        functools.partial(_fcum_kernel, seq=seq),
        grid=(m // seq,),
        in_specs=[pl.BlockSpec((seq, LANES), lambda b: (b, col_block)),
                  pl.BlockSpec((1, LANES), lambda b: (0, 0))],
        out_specs=pl.BlockSpec((seq, LANES), lambda b: (b, 0)),
        out_shape=jax.ShapeDtypeStruct((m, LANES), BF16),
        compiler_params=_cparams(("parallel",)),
        name="fox_cumsum",
    )(proj, fb)


def _attn_kernel(*refs, tq, tk, nblk, mode):
    if mode == "fox":
        q_ref, k_ref, ka_ref, vt_ref, o_ref, vp_scr, acc_scr, st_scr, p_scr, qc_scr = refs
    else:
        q_ref, ns_ref, k_ref, vt_ref, o_ref, vp_scr, acc_scr, st_scr, p_scr, qc_scr = refs
    p = pl.program_id(1)
    qi = pl.program_id(2)
    half = LANES // 2
    n_sub = tq // tk

    @pl.when(qi == 0)
    def _():
        ones = jnp.ones((DENOM_ROWS, tk), BF16)
        for jb in range(nblk):
            for h in range(2):
                vt = vt_ref[h * half:(h + 1) * half, jb * tk:(jb + 1) * tk]
                vp_scr[h, jb] = jnp.concatenate([vt, ones], axis=0)

    lane_q = lax.broadcasted_iota(jnp.int32, (tq, LANES), 1)
    lane_k = lax.broadcasted_iota(jnp.int32, (tk, LANES), 1)
    q = q_ref[...]
    slots = LANES // MOBA_HEADS
    for h in range(2):
        head_lanes = (lane_q < half) if h == 0 else (lane_q >= half)
        qh = jnp.where(head_lanes, q, jnp.zeros_like(q))
        if mode == "fox":
            first = 3 * (2 * p + h)
            aux = jnp.where((lane_q >= first) & (lane_q < first + 3), 1.0, 0.0).astype(BF16)
        else:
            aux = ns_ref[...]
        qc_scr[h] = jnp.concatenate([qh, aux], axis=1)

    def scores(j):
        r0 = pl.multiple_of(j * tk, tk)
        kb = k_ref[pl.ds(r0, tk), :]
        out = []
        for h in range(2):
            if mode == "fox":
                ka = ka_ref[pl.ds(r0, tk), :]
            else:
                ka = jnp.where(lane_k == (2 * p + h) * slots + j, NEG_BIG, 0.0).astype(BF16)
            out.append(_dot_nt(jnp.concatenate([kb, ka], axis=1), qc_scr[h]))
        return out

    def advance(j, ms, mask, prefetch):
        nxt = scores(j + 1) if prefetch else None
        prev = jnp.maximum(j - 1, 0)
        pv = [_dot(vp_scr[h, prev], p_scr[h]) for h in range(2)]
        ms2 = []
        for h in range(2):
            st = st_scr[h]
            if mask is not None:
                st = jnp.where(mask, st, NEG_BIG)
            m_new = jnp.maximum(ms[h], jnp.max(st, axis=0, keepdims=True))
            alpha = jnp.exp(ms[h] - m_new)
            pt = jnp.exp(st - m_new).astype(BF16)
            acc_scr[h] = alpha * (acc_scr[h] + pv[h])
            p_scr[h] = pt
            ms2.append(m_new)
        if prefetch:
            for h in range(2):
                st_scr[h] = nxt[h]
        return tuple(ms2)

    acc_scr[...] = jnp.zeros(acc_scr.shape, F32)
    p_scr[...] = jnp.zeros(p_scr.shape, BF16)
    first_scores = scores(0)
    for h in range(2):
        st_scr[h] = first_scores[h]
    m0 = jnp.full((1, tq), -jnp.inf, F32)
    first_diag = qi * n_sub

    def body(i, ms):
        for s in range(n_sub):
            ms = advance(i * n_sub + s, ms, None, True)
        return ms

    ms = lax.fori_loop(0, qi, body, (m0, m0))
    key_i = lax.broadcasted_iota(jnp.int32, (tk, tq), 0)
    qry_i = lax.broadcasted_iota(jnp.int32, (tk, tq), 1)
    for s in range(n_sub):
        ms = advance(first_diag + s, ms, (key_i + s * tk) <= qry_i, s + 1 < n_sub)
    last = first_diag + n_sub - 1
    outs = []
    for h in range(2):
        a = acc_scr[h] + _dot(vp_scr[h, last], p_scr[h])
        outs.append(a[0:half, :] / a[half:half + 1, :])
    o_ref[...] = jnp.concatenate(outs, axis=0).T.astype(o_ref.dtype)


def _block_attention(mode, q_arr, q_col, aux_arr, k_arr, k_col, vt_arr, vt_row, seq, tq, tk):
    m = q_arr.shape[0]
    nb = m // seq
    nq = seq // tq
    pairs = FOX_HEADS // 2
    q_spec = pl.BlockSpec((tq, LANES), lambda b, p, i: (b * nq + i, q_col + p))
    k_spec = pl.BlockSpec((seq, LANES), lambda b, p, i: (b, k_col + p))
    vt_spec = pl.BlockSpec((LANES, seq), lambda b, p, i: (vt_row + p, b))
    if mode == "fox":
        in_specs = [q_spec, k_spec, pl.BlockSpec((seq, LANES), lambda b, p, i: (b, 0)), vt_spec]
        args = (q_arr, k_arr, aux_arr, vt_arr)
    else:
        in_specs = [q_spec, pl.BlockSpec((tq, LANES), lambda b, p, i: (b * nq + i, 0)), k_spec, vt_spec]
        args = (q_arr, aux_arr, k_arr, vt_arr)
    return pl.pallas_call(
        functools.partial(_attn_kernel, tq=tq, tk=tk, nblk=seq // tk, mode=mode),
        grid=(nb, pairs, nq),
        in_specs=in_specs,
        out_specs=pl.BlockSpec((tq, LANES), lambda b, p, i: (b * nq + i, p)),
        out_shape=jax.ShapeDtypeStruct((m, pairs * LANES), BF16),
        scratch_shapes=[pltpu.VMEM((2, seq // tk, HEAD_DIM + DENOM_ROWS, tk), BF16),
                        pltpu.VMEM((2, HEAD_DIM + DENOM_ROWS, tq), F32),
                        pltpu.VMEM((2, tk, tq), F32), pltpu.VMEM((2, tk, tq), BF16),
                        pltpu.VMEM((2, tq, 2 * LANES), BF16)],
        compiler_params=_cparams(("arbitrary", "arbitrary", "arbitrary")),
        name=mode + "_attn",
    )(*args)


def _rope_tables(seq):
    half = ROPE_DIM // 2
    inv_freq = ROPE_THETA ** (-jnp.arange(half, dtype=F32) / half)
    ang = jnp.arange(seq, dtype=F32)[:, None] * inv_freq[None, :]
    cos, sin = jnp.cos(ang), jnp.sin(ang)
    ones = jnp.ones((seq, HEAD_DIM - ROPE_DIM), F32)
    cos_h = jnp.concatenate([cos, cos, ones], axis=1)
    sin_h = jnp.concatenate([-sin, sin, 0.0 * ones], axis=1)
    return jnp.tile(cos_h, (1, 2)), jnp.tile(sin_h, (1, 2))


def _rotary(x, cos_t, sin_t):
    half = ROPE_DIM // 2
    lane = lax.broadcasted_iota(jnp.int32, (x.shape[0], LANES), 1)
    first_half = (lane & (HEAD_DIM - 1)) < half
    outs = []
    for j in range(x.shape[1] // LANES):
        xc = x[:, j * LANES:(j + 1) * LANES]
        up = pltpu.roll(xc, LANES - half, axis=1)
        down = pltpu.roll(xc, half, axis=1)
        outs.append(xc * cos_t + jnp.where(first_half, up, down) * sin_t)
    return jnp.concatenate(outs, axis=1)


def _moba_k_kernel(k_ref, cos_ref, sin_ref, ko_ref, km_ref):
    kr = _rotary(k_ref[...], cos_ref[...], sin_ref[...])
    ko_ref[...] = kr.astype(ko_ref.dtype)
    km_ref[0] = jnp.mean(kr, axis=0, keepdims=True)


def _moba_k_prep(proj, k_col, cos_t, sin_t, seq):
    m = proj.shape[0]
    t = MOBA_BLOCK
    nblk = seq // t
    return pl.pallas_call(
        _moba_k_kernel,
        grid=(m // t,),
        in_specs=[pl.BlockSpec((t, MOBA_WIDTH), lambda i: (i, k_col)),
                  pl.BlockSpec((t, LANES), lambda i: (i % nblk, 0)),
                  pl.BlockSpec((t, LANES), lambda i: (i % nblk, 0))],
        out_specs=[pl.BlockSpec((t, MOBA_WIDTH), lambda i: (i, 0)),
                   pl.BlockSpec((1, 1, MOBA_WIDTH), lambda i: (i, 0, 0))],
        out_shape=[jax.ShapeDtypeStruct((m, MOBA_WIDTH), BF16),
                   jax.ShapeDtypeStruct((m // t, 1, MOBA_WIDTH), F32)],
        compiler_params=_cparams(("parallel",)),
        name="moba_k_prep",
    )(proj, cos_t, sin_t)


def _moba_q_kernel(q_ref, cos_ref, sin_ref, km_ref, qo_ref, ns_ref, *, nblk):
    own = pl.program_id(0) % nblk
    qr = _rotary(q_ref[...], cos_ref[...], sin_ref[...])
    qo_ref[...] = (qr * QK_SCALE).astype(qo_ref.dtype)
    gate_t = _dot_f32_nt(km_ref[0], qr)
    slots = LANES // MOBA_HEADS
    n = lax.broadcasted_iota(jnp.int32, (slots, gate_t.shape[1]), 0)
    flags = []
    for h in range(MOBA_HEADS):
        g = gate_t[h * slots:(h + 1) * slots, :]
        cnt = jnp.zeros(g.shape, jnp.int32)
        for cand in range(slots):
            other = g[cand:cand + 1, :]
            beats = (other > g) | ((other == g) & (n > cand))
            cnt = cnt + jnp.where(beats, jnp.where(cand < own, 1, 0), 0)
        selected = (n == own) | ((n < own) & (cnt < MOBA_TOPK))
        flags.append(jnp.where(selected, 0.0, 1.0))
    ns_ref[...] = jnp.concatenate(flags, axis=0).T.astype(ns_ref.dtype)


def _moba_q_prep(proj, q_col, cos_t, sin_t, km_mat, seq):
    m = proj.shape[0]
    t = MOBA_BLOCK
    nblk = seq // t
    return pl.pallas_call(
        functools.partial(_moba_q_kernel, nblk=nblk),
        grid=(m // t,),
        in_specs=[pl.BlockSpec((t, MOBA_WIDTH), lambda i: (i, q_col)),
                  pl.BlockSpec((t, LANES), lambda i: (i % nblk, 0)),
                  pl.BlockSpec((t, LANES), lambda i: (i % nblk, 0)),
                  pl.BlockSpec((1, LANES, MOBA_WIDTH), lambda i: (i // nblk, 0, 0))],
        out_specs=[pl.BlockSpec((t, MOBA_WIDTH), lambda i: (i, 0)),
                   pl.BlockSpec((t, LANES), lambda i: (i, 0))],
        out_shape=[jax.ShapeDtypeStruct((m, MOBA_WIDTH), BF16),
                   jax.ShapeDtypeStruct((m, LANES), BF16)],
        compiler_params=_cparams(("parallel",)),
        name="moba_q_prep",
    )(proj, cos_t, sin_t, km_mat)


def _pair_cols(x, h0, h1, shape):
    lane = lax.broadcasted_iota(jnp.int32, shape, 1)
    a = jnp.broadcast_to(x[:, h0:h0 + 1], shape)
    b = jnp.broadcast_to(x[:, h1:h1 + 1], shape)
    return jnp.where(lane < SSM_HEAD_DIM, a, b)


def _ssd_kernel(in_ref, cw_ref, cb_ref, dtb_ref, alog_ref, dsk_ref, nw_ref, o_ref, xpad_scr, st_scr):
    t = SSM_CHUNK
    c = pl.program_id(1)
    halo = SUBLANES

    @pl.when(c == 0)
    def _():
        xpad_scr[0:halo, :] = jnp.zeros((halo, SSM_CONV_DIM), F32)
        st_scr[...] = jnp.zeros(st_scr.shape, F32)

    xpad_scr[halo:halo + t, :] = in_ref[:, 0:SSM_CONV_DIM]
    conv = cb_ref[...] + cw_ref[0:1, :] * xpad_scr[pl.ds(halo - SSM_CONV + 1, t), :]
    for k in range(1, SSM_CONV):
        conv = conv + cw_ref[k:k + 1, :] * xpad_scr[pl.ds(halo - SSM_CONV + 1 + k, t), :]
    xpad_scr[0:halo, :] = xpad_scr[t:t + halo, :]
    xbc = conv * _sigmoid(conv)
    z = in_ref[:, SSM_CONV_DIM:SSM_CONV_DIM + SSM_WIDTH]
    dt = _softplus(in_ref[:, SSM_CONV_DIM + SSM_WIDTH:] + dtb_ref[...])
    a = -jnp.exp(alog_ref[...])
    acum = _tri_cumsum(dt * a)
    acum_t = acum.T
    tot = acum[t - 1:t, :]
    decay_end = jnp.exp(tot - acum)
    decay_in = jnp.exp(acum)
    exp_tot = jnp.exp(tot)
    row = lax.broadcasted_iota(jnp.int32, (t, t), 0)
    col = lax.broadcasted_iota(jnp.int32, (t, t), 1)
    causal = row >= col
    lane = lax.broadcasted_iota(jnp.int32, (t, LANES), 1)
    lane1 = lax.broadcasted_iota(jnp.int32, (1, LANES), 1)
    pair_shape = (t, LANES)
    rep = SSM_HEADS // SSM_GROUPS
    b_off = SSM_WIDTH
    c_off = SSM_WIDTH + SSM_GROUPS * SSM_STATE
    ys = []
    for g in range(SSM_GROUPS):
        bm = xbc[:, b_off + g * SSM_STATE:b_off + (g + 1) * SSM_STATE]
        cm = xbc[:, c_off + g * SSM_STATE:c_off + (g + 1) * SSM_STATE].astype(BF16)
        cb = _dot_nt(cm, bm.astype(BF16))
        bm_t = bm.T.astype(BF16)
        for pr in range(rep // 2):
            j = g * (rep // 2) + pr
            h0, h1 = 2 * j, 2 * j + 1
            xp = xbc[:, j * LANES:(j + 1) * LANES]
            xdt = xp * _pair_cols(dt, h0, h1, pair_shape)
            xdt_b = xdt.astype(BF16)
            yd = []
            for h in (h0, h1):
                diff = acum[:, h:h + 1] - acum_t[h:h + 1, :]
                w = cb * jnp.exp(jnp.where(causal, diff, NEG_BIG))
                yd.append(_dot(w.astype(BF16), xdt_b))
            y_diag = jnp.where(lane < SSM_HEAD_DIM, yd[0], yd[1])
            prev = st_scr[j]
            y_off = _dot(cm, prev.astype(BF16)) * _pair_cols(decay_in, h0, h1, pair_shape)
            xw = (xdt * _pair_cols(decay_end, h0, h1, pair_shape)).astype(BF16)
            scale = jnp.where(lane1 < SSM_HEAD_DIM, exp_tot[:, h0:h0 + 1], exp_tot[:, h1:h1 + 1])
            st_scr[j] = prev * scale + _dot(bm_t, xw)
            ys.append(y_diag + y_off + dsk_ref[:, j * LANES:(j + 1) * LANES] * xp)
    y = jnp.concatenate(ys, axis=1)
    y = y * (z * _sigmoid(z))
    gw = SSM_WIDTH // SSM_GROUPS
    outs = []
    for g in range(SSM_GROUPS):
        seg = y[:, g * gw:(g + 1) * gw]
        ms = jnp.mean(seg * seg, axis=-1, keepdims=True)
        outs.append(seg * lax.rsqrt(ms + NORM_EPS) * nw_ref[:, g * gw:(g + 1) * gw])
    o_ref[...] = jnp.concatenate(outs, axis=1).astype(o_ref.dtype)


def _ssd(proj, conv_w, conv_b, dt_bias, a_log, d_skip, norm_w, seq):
    m, wtot = proj.shape
    t = SSM_CHUNK
    nc = seq // t
    pad = LANES - SSM_HEADS
    dtb = jnp.pad(dt_bias, (0, pad)).reshape(1, LANES)
    alog = jnp.pad(a_log, (0, pad)).reshape(1, LANES)
    dsk = jnp.repeat(d_skip, SSM_HEAD_DIM).reshape(1, SSM_WIDTH)
    const = lambda b, c: (0, 0)
    return pl.pallas_call(
        _ssd_kernel,
        grid=(m // seq, nc),
        in_specs=[pl.BlockSpec((t, wtot), lambda b, c: (b * nc + c, 0)),
                  pl.BlockSpec((SSM_CONV, SSM_CONV_DIM), const),
                  pl.BlockSpec((1, SSM_CONV_DIM), const),
                  pl.BlockSpec((1, LANES), const),
                  pl.BlockSpec((1, LANES), const),
                  pl.BlockSpec((1, SSM_WIDTH), const),
                  pl.BlockSpec((1, SSM_WIDTH), const)],
        out_specs=pl.BlockSpec((t, SSM_WIDTH), lambda b, c: (b * nc + c, 0)),
        out_shape=jax.ShapeDtypeStruct((m, SSM_WIDTH), BF16),
        scratch_shapes=[pltpu.VMEM((t + SUBLANES, SSM_CONV_DIM), F32),
                        pltpu.VMEM((SSM_HEADS // 2, SSM_STATE, LANES), F32)],
        compiler_params=_cparams(("parallel", "arbitrary")),
        name="ssd",
    )(proj, conv_w, conv_b.reshape(1, SSM_CONV_DIM), dtb, alog, dsk, norm_w.reshape(1, SSM_WIDTH))


def _merge_kernel(ya_ref, yb_ref, yc_ref, yd_ref, g_ref, x_ref, g1_ref,
                  wa_ref, wb_ref, wc_ref, wd_ref, wo_ref, o_ref):
    d = x_ref.shape[1]
    gate = lambda i: _sigmoid(g_ref[:, i * d:(i + 1) * d].astype(F32))
    merged = gate(0) * _dot(ya_ref[...], wa_ref[...])
    merged = merged + gate(1) * _dot(yb_ref[...], wb_ref[...])
    merged = merged + gate(2) * _dot(yc_ref[...], wc_ref[...])
    merged = merged + gate(3) * _dot(yd_ref[...], wd_ref[...])
    o_ref[...] = x_ref[...] + g1_ref[0] * _dot(merged.astype(BF16), wo_ref[...])


def _merge(ya, yb, yc, yd, gates, x, mod, k_gate, wa, wb, wc, wd, wo, seq, tm):
    m, d = x.shape
    nb = m // seq
    rows = lambda i: (i, 0)
    const = lambda i: (0, 0)
    return pl.pallas_call(
        _merge_kernel,
        grid=(m // tm,),
        in_specs=[pl.BlockSpec((tm, ya.shape[1]), rows),
                  pl.BlockSpec((tm, yb.shape[1]), rows),
                  pl.BlockSpec((tm, yc.shape[1]), rows),
                  pl.BlockSpec((tm, yd.shape[1]), rows),
                  pl.BlockSpec((tm, N_BRANCH * d), rows),
                  pl.BlockSpec((tm, d), rows),
                  pl.BlockSpec((1, 1, d), lambda i: (k_gate * nb + (i * tm) // seq, 0, 0)),
                  pl.BlockSpec(wa.shape, const),
                  pl.BlockSpec(wb.shape, const),
                  pl.BlockSpec(wc.shape, const),
                  pl.BlockSpec(wd.shape, const),
                  pl.BlockSpec(wo.shape, const)],
        out_specs=pl.BlockSpec((tm, d), rows),
        out_shape=jax.ShapeDtypeStruct((m, d), F32),
        compiler_params=_cparams(("parallel",)),
        name="merge",
    )(ya, yb, yc, yd, gates, x, mod, wa, wb, wc, wd, wo)


def _ffn_kernel(x_ref, nw_ref, sc_ref, sh_ref, g_ref, w1_ref, w2_ref, o_ref, h_scr, acc_scr):
    j = pl.program_id(1)

    @pl.when(j == 0)
    def _():
        h_scr[...] = _rms_mod(x_ref[...], nw_ref[...], sc_ref[0], sh_ref[0]).astype(BF16)
        acc_scr[...] = jnp.zeros(acc_scr.shape, F32)

    a = jnp.maximum(_dot(h_scr[...], w1_ref[...]), 0.0)
    acc_scr[...] += _dot((a * a).astype(BF16), w2_ref[...])

    @pl.when(j == pl.num_programs(1) - 1)
    def _():
        o_ref[...] = x_ref[...] + g_ref[0] * acc_scr[...]


def _ffn(x, nw, mod, k_sc, k_sh, k_gate, w1, w2, seq, tm, tf):
    m, d = x.shape
    nb = m // seq
    dff = w1.shape[1]
    modrow = lambda k: (lambda i, j: (k * nb + (i * tm) // seq, 0, 0))
    return pl.pallas_call(
        _ffn_kernel,
        grid=(m // tm, dff // tf),
        in_specs=[pl.BlockSpec((tm, d), lambda i, j: (i, 0)),
                  pl.BlockSpec((1, d), lambda i, j: (0, 0)),
                  pl.BlockSpec((1, 1, d), modrow(k_sc)),
                  pl.BlockSpec((1, 1, d), modrow(k_sh)),
                  pl.BlockSpec((1, 1, d), modrow(k_gate)),
                  pl.BlockSpec((d, tf), lambda i, j: (0, j)),
                  pl.BlockSpec((tf, d), lambda i, j: (j, 0))],
        out_specs=pl.BlockSpec((tm, d), lambda i, j: (i, 0)),
        out_shape=jax.ShapeDtypeStruct((m, d), F32),
        scratch_shapes=[pltpu.VMEM((tm, d), BF16), pltpu.VMEM((tm, d), F32)],
        compiler_params=_cparams(("parallel", "arbitrary")),
        name="ffn",
    )(x, nw.reshape(1, d), mod, mod, mod, w1, w2)


def _pad_rows(w, rows):
    return jnp.pad(w, ((0, rows - w.shape[0]), (0, 0)))


def _split_w_in(w_in, l):
    wt = jnp.transpose(w_in, (2, 0, 1))[:, l, :]
    o = 0
    uv = wt[o:o + 2 * GMLP_WIDTH]; o += 2 * GMLP_WIDTH
    fox_qk = wt[o:o + 2 * FOX_WIDTH]; o += 2 * FOX_WIDTH
    fox_v = wt[o:o + FOX_WIDTH]; o += FOX_WIDTH
    fox_f = wt[o:o + FOX_HEADS]; o += FOX_HEADS
    moba_qk = wt[o:o + 2 * MOBA_WIDTH]; o += 2 * MOBA_WIDTH
    moba_v = wt[o:o + MOBA_WIDTH]; o += MOBA_WIDTH
    z = wt[o:o + SSM_WIDTH]; o += SSM_WIDTH
    xbc = wt[o:o + SSM_CONV_DIM]; o += SSM_CONV_DIM
    dt = wt[o:o + SSM_HEADS]; o += SSM_HEADS
    gates = wt[o:]
    w_v = jnp.concatenate([fox_v, moba_v], axis=0)
    w_mqkf = jnp.concatenate([moba_qk, _pad_rows(fox_f, LANES)], axis=0)
    w_ssd = jnp.concatenate([xbc, z, _pad_rows(dt, LANES)], axis=0)
    return [g.astype(BF16) for g in (uv, fox_qk, w_v, w_mqkf, w_ssd, gates)]


def _tile_rows(seq, want):
    return min(want, seq)


def kernel(x, c, ada_w, ada_b, norm_mix_w, w_in, gmlp_ln_w, gmlp_ln_b, gmlp_ws, gmlp_bs, fox_f_bias, ssm_conv_w, ssm_conv_b, ssm_dt_bias, ssm_a_log, ssm_d, ssm_norm_w, w_branch_a, w_branch_b, w_branch_c, w_branch_d, w_out, norm_mlp_w, mlp_w1, mlp_w2, final_norm_w):
    nb, seq, d = x.shape
    m = nb * seq
    depth = ada_w.shape[0]
    tm = _tile_rows(seq, 512)
    tm_proj = _tile_rows(seq, 1024)
    tq_attn = _tile_rows(seq, ATTN_Q_TILE)
    pairs = FOX_HEADS // 2
    nblk = seq // MOBA_BLOCK
    slots = LANES // MOBA_HEADS
    assert seq % MOBA_BLOCK == 0 and nblk <= slots and seq % SSM_CHUNK == 0

    mod_all = _modulation(c, ada_w, ada_b)[:, :nb]
    mod_all = mod_all.reshape(depth, nb, 6, d).transpose(0, 2, 1, 3).reshape(depth, 6 * nb, 1, d)
    cos_t, sin_t = _rope_tables(seq)
    head_eye = jnp.eye(MOBA_HEADS, dtype=F32)

    xf = x.reshape(m, d)
    for l in range(depth):
        mod = mod_all[l]
        w_uv, w_fqk, w_v, w_mqkf, w_ssd, w_gate = _split_w_in(w_in, l)
        h = _norm_mod(xf, norm_mix_w[l], mod, 1, 0, seq, tm)
        p_uv = _matmul(h, w_uv, F32, tm_proj, w_uv.shape[0], "proj_uv")
        p_fqk = _matmul(h, w_fqk, BF16, tm_proj, w_fqk.shape[0], "proj_fox_qk", FOX_WIDTH, QK_SCALE)
        p_vt = _matmul_t(h, w_v, BF16, tm_proj, "proj_vt")
        p_mqkf = _matmul(h, w_mqkf, F32, tm_proj, w_mqkf.shape[0], "proj_mqkf")
        p_ssd = _matmul(h, w_ssd, F32, tm_proj, w_ssd.shape[0], "proj_ssd")
        p_gate = _matmul(h, w_gate, BF16, tm_proj, 2 * d, "proj_gate")

        y_a = _gmlp(p_uv, gmlp_ln_w[l], gmlp_ln_b[l], gmlp_ws[l], gmlp_bs[l], tm)

        k_aux = _forget_cumsum(p_mqkf, 2 * MOBA_WIDTH // LANES, fox_f_bias[l], seq)
        y_b = _block_attention("fox", p_fqk, 0, k_aux, p_fqk, pairs, p_vt, 0, seq, tq_attn, MOBA_BLOCK)

        k_rot, k_mean = _moba_k_prep(p_mqkf, 1, cos_t, sin_t, seq)
        km = jnp.pad(k_mean.reshape(nb, nblk, MOBA_HEADS, HEAD_DIM), ((0, 0), (0, slots - nblk), (0, 0), (0, 0)))
        km_mat = jnp.einsum('bnhd,kh->bknhd', km, head_eye).reshape(nb, LANES, MOBA_WIDTH)
        q_rot, not_sel = _moba_q_prep(p_mqkf, 0, cos_t, sin_t, km_mat, seq)
        y_c = _block_attention("moba", q_rot, 0, not_sel, k_rot, 0, p_vt, pairs, seq, tq_attn, MOBA_BLOCK)

        y_d = _ssd(p_ssd, ssm_conv_w[l], ssm_conv_b[l], ssm_dt_bias[l], ssm_a_log[l], ssm_d[l],
                   ssm_norm_w[l], seq)

        xf = _merge(y_a, y_b, y_c, y_d, p_gate, xf, mod, 2,
                    w_branch_a[l].astype(BF16), w_branch_b[l].astype(BF16), w_branch_c[l].astype(BF16),
                    w_branch_d[l].astype(BF16), w_out[l].astype(BF16), seq, tm)
        xf = _ffn(xf, norm_mlp_w[l], mod, 4, 3, 5, mlp_w1[l].astype(BF16), mlp_w2[l].astype(BF16),
                  seq, _tile_rows(seq, 1024), 1024)
    return _final_norm(xf, final_norm_w, tm).reshape(nb, seq, d)
```

```python
import functools
import math

import jax
import jax.numpy as jnp
from jax import lax
from jax.experimental import pallas as pl
from jax.experimental.pallas import tpu as pltpu

F32 = jnp.float32
BF16 = jnp.bfloat16

HEAD_DIM = 64
NORM_EPS = 1e-6
GMLP_GROUPS = 8
GMLP_WIDTH = GMLP_GROUPS * HEAD_DIM
GMLP_CHUNK = 128
FOX_HEADS = 8
FOX_WIDTH = FOX_HEADS * HEAD_DIM
MOBA_HEADS = 8
MOBA_WIDTH = MOBA_HEADS * HEAD_DIM
MOBA_BLOCK = 256
MOBA_TOPK = 3
ROPE_THETA = 500000.0
ROPE_DIM = HEAD_DIM // 4
SSM_HEADS = 12
SSM_HEAD_DIM = 64
SSM_WIDTH = SSM_HEADS * SSM_HEAD_DIM
SSM_GROUPS = 2
SSM_STATE = 128
SSM_CONV = 4
SSM_CHUNK = 128
SSM_CONV_DIM = SSM_WIDTH + 2 * SSM_GROUPS * SSM_STATE
N_BRANCH = 4

LANES = 128
SUBLANES = 8
NEG_BIG = -1e30
VMEM_LIMIT = 48 * 1024 * 1024
ATTN_Q_TILE = 512
DENOM_ROWS = 64
LOOP_STAGES = 2
QK_SCALE = HEAD_DIM ** -0.5


def _cparams(sem):
    return pltpu.CompilerParams(dimension_semantics=sem, vmem_limit_bytes=VMEM_LIMIT)


def _sigmoid(x):
    return 1.0 / (1.0 + jnp.exp(-x))


def _softplus(x):
    return jnp.maximum(x, 0.0) + jnp.log(1.0 + jnp.exp(-jnp.abs(x)))


def _dot(a, b):
    return jnp.dot(a, b, preferred_element_type=F32)


def _dot_nt(a, b):
    return lax.dot_general(a, b, (((1,), (1,)), ((), ())), preferred_element_type=F32)


def _keep_high_half(x):
    bits = lax.bitcast_convert_type(x, jnp.uint32) & jnp.uint32(0xFFFF0000)
    return lax.bitcast_convert_type(bits, F32)


def _split3(x):
    hi = _keep_high_half(x)
    r = x - hi
    mid = _keep_high_half(r)
    lo = r - mid
    return hi.astype(BF16), mid.astype(BF16), lo.astype(BF16)


def _tri_cumsum(x):
    t = x.shape[0]
    row = lax.broadcasted_iota(jnp.int32, (t, t), 0)
    col = lax.broadcasted_iota(jnp.int32, (t, t), 1)
    tri = jnp.where(row >= col, 1.0, 0.0).astype(BF16)
    hi, mid, lo = _split3(x)
    return _dot(tri, hi) + _dot(tri, mid) + _dot(tri, lo)


def _six_pass(dot, a, b):
    ah, am, al = _split3(a)
    bh, bm, bl = _split3(b)
    small = dot(am, bm) + dot(ah, bl) + dot(al, bh)
    return dot(ah, bh) + (dot(ah, bm) + dot(am, bh)) + small


def _dot_f32(a, b):
    return _six_pass(_dot, a, b)


def _dot_f32_nt(a, b):
    return _six_pass(_dot_nt, a, b)


def _mod_kernel(c_ref, w_ref, b_ref, o_ref):
    c = c_ref[...]
    ca = c * _sigmoid(c)
    o_ref[0] = _dot_f32(ca, w_ref[0]) + b_ref[0]


def _modulation(c, ada_w, ada_b):
    nb, d = c.shape
    nl = ada_w.shape[0]
    bp = -(-nb // SUBLANES) * SUBLANES
    cp = jnp.pad(c, ((0, bp - nb), (0, 0)))
    return pl.pallas_call(
        _mod_kernel,
        grid=(nl, 6),
        in_specs=[pl.BlockSpec((bp, d), lambda l, k: (0, 0)),
                  pl.BlockSpec((1, d, d), lambda l, k: (l, 0, k)),
                  pl.BlockSpec((1, 1, d), lambda l, k: (l, 0, k))],
        out_specs=pl.BlockSpec((1, bp, d), lambda l, k: (l, 0, k)),
        out_shape=jax.ShapeDtypeStruct((nl, bp, 6 * d), F32),
        compiler_params=_cparams(("parallel", "parallel")),
        name="adaln_mod",
    )(cp, ada_w, ada_b.reshape(nl, 1, 6 * d))


def _rms_mod(x, nw, sc, sh):
    ms = jnp.mean(x * x, axis=-1, keepdims=True)
    xn = x * lax.rsqrt(ms + NORM_EPS)
    return xn * nw * (1.0 + sc) + sh


def _norm_kernel(x_ref, nw_ref, sc_ref, sh_ref, o_ref):
    o_ref[...] = _rms_mod(x_ref[...], nw_ref[...], sc_ref[0], sh_ref[0]).astype(o_ref.dtype)


def _norm_mod(x, nw, mod, k_sc, k_sh, seq, tm):
    m, d = x.shape
    nb = m // seq
    return pl.pallas_call(
        _norm_kernel,
        grid=(m // tm,),
        in_specs=[pl.BlockSpec((tm, d), lambda i: (i, 0)),
                  pl.BlockSpec((1, d), lambda i: (0, 0)),
                  pl.BlockSpec((1, 1, d), lambda i: (k_sc * nb + (i * tm) // seq, 0, 0)),
                  pl.BlockSpec((1, 1, d), lambda i: (k_sh * nb + (i * tm) // seq, 0, 0))],
        out_specs=pl.BlockSpec((tm, d), lambda i: (i, 0)),
        out_shape=jax.ShapeDtypeStruct((m, d), BF16),
        compiler_params=_cparams(("parallel",)),
        name="norm_mod",
    )(x, nw.reshape(1, d), mod, mod)


def _final_norm_kernel(x_ref, nw_ref, o_ref):
    x = x_ref[...]
    ms = jnp.mean(x * x, axis=-1, keepdims=True)
    o_ref[...] = x * lax.rsqrt(ms + NORM_EPS) * nw_ref[...]


def _final_norm(x, nw, tm):
    m, d = x.shape
    return pl.pallas_call(
        _final_norm_kernel,
        grid=(m // tm,),
        in_specs=[pl.BlockSpec((tm, d), lambda i: (i, 0)),
                  pl.BlockSpec((1, d), lambda i: (0, 0))],
        out_specs=pl.BlockSpec((tm, d), lambda i: (i, 0)),
        out_shape=jax.ShapeDtypeStruct((m, d), F32),
        compiler_params=_cparams(("parallel",)),
        name="final_norm",
    )(x, nw.reshape(1, d))


def _mm_kernel(a_ref, wt_ref, o_ref, *, lead_cols, lead_scale):
    out = _dot_nt(a_ref[...], wt_ref[...])
    if lead_cols:
        out = jnp.concatenate([out[:, :lead_cols] * lead_scale, out[:, lead_cols:]], axis=1)
    o_ref[...] = out.astype(o_ref.dtype)


def _matmul(a, wt, out_dtype, tm, tn, name, lead_cols=0, lead_scale=1.0):
    m, k = a.shape
    n = wt.shape[0]
    assert lead_cols == 0 or tn == n
    return pl.pallas_call(
        functools.partial(_mm_kernel, lead_cols=lead_cols, lead_scale=lead_scale),
        grid=(n // tn, m // tm),
        in_specs=[pl.BlockSpec((tm, k), lambda j, i: (i, 0)),
                  pl.BlockSpec((tn, k), lambda j, i: (j, 0))],
        out_specs=pl.BlockSpec((tm, tn), lambda j, i: (i, j)),
        out_shape=jax.ShapeDtypeStruct((m, n), out_dtype),
        compiler_params=_cparams(("parallel", "parallel")),
        name=name,
    )(a, wt)


def _mm_t_kernel(a_ref, wt_ref, o_ref):
    o_ref[...] = _dot_nt(wt_ref[...], a_ref[...]).astype(o_ref.dtype)


def _matmul_t(a, wt, out_dtype, tm, name):
    m, k = a.shape
    n = wt.shape[0]
    return pl.pallas_call(
        _mm_t_kernel,
        grid=(m // tm,),
        in_specs=[pl.BlockSpec((tm, k), lambda i: (i, 0)),
                  pl.BlockSpec((n, k), lambda i: (0, 0))],
        out_specs=pl.BlockSpec((n, tm), lambda i: (0, i)),
        out_shape=jax.ShapeDtypeStruct((n, m), out_dtype),
        compiler_params=_cparams(("parallel",)),
        name=name,
    )(a, wt)


def _gelu_tanh(x):
    c = math.sqrt(2.0 / math.pi)
    return 0.5 * x * (1.0 + jnp.tanh(c * (x + 0.044715 * (x * x * x))))


def _gmlp_kernel(uv_ref, lnw_ref, lnb_ref, ws_ref, bs_ref, o_ref, *, tm):
    g = _gelu_tanh(uv_ref[...])
    u = g[:, :GMLP_WIDTH]
    v = g[:, GMLP_WIDTH:]
    mu = jnp.mean(v, axis=-1, keepdims=True)
    vc = v - mu
    var = jnp.mean(vc * vc, axis=-1, keepdims=True)
    vn = (vc * lax.rsqrt(var + NORM_EPS) * lnw_ref[...] + lnb_ref[...]).astype(BF16)
    t = GMLP_CHUNK
    row = lax.broadcasted_iota(jnp.int32, (t, t), 0)
    col = lax.broadcasted_iota(jnp.int32, (t, t), 1)
    causal = row >= col
    ws = [jnp.where(causal, ws_ref[i], 0.0).astype(BF16) for i in range(GMLP_GROUPS)]
    first_head = lax.broadcasted_iota(jnp.int32, (t, LANES), 1) < HEAD_DIM
    for c in range(tm // t):
        rows = slice(c * t, (c + 1) * t)
        for p in range(GMLP_WIDTH // LANES):
            cols = slice(p * LANES, (p + 1) * LANES)
            vp = vn[rows, cols]
            mixed = jnp.where(first_head, _dot(ws[2 * p], vp), _dot(ws[2 * p + 1], vp))
            o_ref[rows, cols] = (u[rows, cols] * (mixed + bs_ref[:, cols])).astype(o_ref.dtype)


def _gmlp(uv, ln_w, ln_b, ws, bs, tm):
    m = uv.shape[0]
    w = GMLP_WIDTH
    bs_full = jnp.repeat(bs.T, HEAD_DIM, axis=1)
    return pl.pallas_call(
        functools.partial(_gmlp_kernel, tm=tm),
        grid=(m // tm,),
        in_specs=[pl.BlockSpec((tm, 2 * w), lambda i: (i, 0)),
                  pl.BlockSpec((1, w), lambda i: (0, 0)),
                  pl.BlockSpec((1, w), lambda i: (0, 0)),
                  pl.BlockSpec((GMLP_GROUPS, GMLP_CHUNK, GMLP_CHUNK), lambda i: (0, 0, 0)),
                  pl.BlockSpec((GMLP_CHUNK, w), lambda i: (0, 0))],
        out_specs=pl.BlockSpec((tm, w), lambda i: (i, 0)),
        out_shape=jax.ShapeDtypeStruct((m, w), BF16),
        compiler_params=_cparams(("parallel",)),
        name="gmlp",
    )(uv, ln_w.reshape(1, w), ln_b.reshape(1, w), ws, bs_full)


def _fcum_kernel(f_ref, b_ref, o_ref, *, seq):
    t = LANES
    r = lax.broadcasted_iota(jnp.int32, (LANES, LANES), 0)
    c = lax.broadcasted_iota(jnp.int32, (LANES, LANES), 1)
    spread = [jnp.where((c == 3 * r + part) & (r < FOX_HEADS), 1.0, 0.0).astype(BF16) for part in range(3)]

    def body(i, carry):
        r0 = pl.multiple_of(i * t, t)
        z = f_ref[pl.ds(r0, t), :] + b_ref[...]
        logf = jnp.minimum(z, 0.0) - jnp.log(1.0 + jnp.exp(-jnp.abs(z)))
        cs = _tri_cumsum(logf) + carry
        hi, mid, lo = _split3(-cs)
        o_ref[pl.ds(r0, t), :] = (_dot(hi, spread[0]) + _dot(mid, spread[1]) + _dot(lo, spread[2])).astype(o_ref.dtype)
        return cs[t - 1:t, :]

    lax.fori_loop(0, seq // t, body, jnp.zeros((1, LANES), F32))


def _forget_cumsum(proj, col_block, f_bias, seq):
    m = proj.shape[0]
    fb = jnp.pad(f_bias, (0, LANES - FOX_HEADS)).reshape(1, LANES)
    return pl.pallas_call(
        functools.partial(_fcum_kernel, seq=seq),
        grid=(m // seq,),
        in_specs=[pl.BlockSpec((seq, LANES), lambda b: (b, col_block)),
                  pl.BlockSpec((1, LANES), lambda b: (0, 0))],
        out_specs=pl.BlockSpec((seq, LANES), lambda b: (b, 0)),
        out_shape=jax.ShapeDtypeStruct((m, LANES), BF16),
        compiler_params=_cparams(("parallel",)),
        name="fox_cumsum",
    )(proj, fb)


def _attn_kernel(*refs, tq, tk, nblk, mode):
    if mode == "fox":
        q_ref, k_ref, ka_ref, vt_ref, o_ref, vp_scr, acc_scr, st_scr, p_scr, qc_scr, m_scr = refs
    else:
        q_ref, ns_ref, k_ref, vt_ref, o_ref, vp_scr, acc_scr, st_scr, p_scr, qc_scr, m_scr = refs
    p = pl.program_id(1)
    qi = pl.program_id(2)
    half = LANES // 2
    n_sub = tq // tk

    @pl.when(qi == 0)
    def _():
        ones = jnp.ones((DENOM_ROWS, tk), BF16)
        for jb in range(nblk):
            for h in range(2):
                vt = vt_ref[h * half:(h + 1) * half, jb * tk:(jb + 1) * tk]
                vp_scr[h, jb] = jnp.concatenate([vt, ones], axis=0)

    lane_q = lax.broadcasted_iota(jnp.int32, (tq, LANES), 1)
    lane_k = lax.broadcasted_iota(jnp.int32, (tk, LANES), 1)
    q = q_ref[...]
    slots = LANES // MOBA_HEADS
    for h in range(2):
        head_lanes = (lane_q < half) if h == 0 else (lane_q >= half)
        qh = jnp.where(head_lanes, q, jnp.zeros_like(q))
        if mode == "fox":
            first = 3 * (2 * p + h)
            aux = jnp.where((lane_q >= first) & (lane_q < first + 3), 1.0, 0.0).astype(BF16)
        else:
            aux = ns_ref[...]
        qc_scr[h] = jnp.concatenate([qh, aux], axis=1)

    def scores(j):
        r0 = pl.multiple_of(j * tk, tk)
        kb = k_ref[pl.ds(r0, tk), :]
        out = []
        for h in range(2):
            if mode == "fox":
                ka = ka_ref[pl.ds(r0, tk), :]
            else:
                ka = jnp.where(lane_k == (2 * p + h) * slots + j, NEG_BIG, 0.0).astype(BF16)
            out.append(_dot_nt(jnp.concatenate([kb, ka], axis=1), qc_scr[h]))
        return out

    def advance(j, mask, prefetch):
        nxt = scores(j + 1) if prefetch else None
        prev = jnp.maximum(j - 1, 0)
        pv = [_dot(vp_scr[h, prev], p_scr[h]) for h in range(2)]
        for h in range(2):
            st = st_scr[h]
            if mask is not None:
                st = jnp.where(mask, st, NEG_BIG)
            m_old = m_scr[h]
            m_new = jnp.maximum(m_old, jnp.max(st, axis=0, keepdims=True))
            alpha = jnp.exp(m_old - m_new)
            pt = jnp.exp(st - m_new).astype(BF16)
            acc_scr[h] = alpha * (acc_scr[h] + pv[h])
            p_scr[h] = pt
            m_scr[h] = m_new
        if prefetch:
            for h in range(2):
                st_scr[h] = nxt[h]

    acc_scr[...] = jnp.zeros(acc_scr.shape, F32)
    p_scr[...] = jnp.zeros(p_scr.shape, BF16)
    first_scores = scores(0)
    for h in range(2):
        st_scr[h] = first_scores[h]
    m_scr[...] = jnp.full(m_scr.shape, -jnp.inf, F32)
    first_diag = qi * n_sub

    trips = first_diag // LOOP_STAGES

    def body(i, carry):
        for s in range(LOOP_STAGES):
            advance(i * LOOP_STAGES + s, None, True)
        return carry

    def rest(i, carry):
        for s in range(n_sub):
            advance(i * n_sub + s, None, True)
        return carry

    lax.fori_loop(0, trips, body, 0)
    lax.fori_loop(trips * (LOOP_STAGES // n_sub), qi, rest, 0)
    key_i = lax.broadcasted_iota(jnp.int32, (tk, tq), 0)
    qry_i = lax.broadcasted_iota(jnp.int32, (tk, tq), 1)
    for s in range(n_sub):
        advance(first_diag + s, (key_i + s * tk) <= qry_i, s + 1 < n_sub)
    last = first_diag + n_sub - 1
    outs = []
    for h in range(2):
        a = acc_scr[h] + _dot(vp_scr[h, last], p_scr[h])
        outs.append(a[0:half, :] / a[half:half + 1, :])
    o_ref[...] = jnp.concatenate(outs, axis=0).T.astype(o_ref.dtype)


def _block_attention(mode, q_arr, q_col, aux_arr, k_arr, k_col, vt_arr, vt_row, seq, tq, tk):
    m = q_arr.shape[0]
    nb = m // seq
    nq = seq // tq
    pairs = FOX_HEADS // 2
    q_spec = pl.BlockSpec((tq, LANES), lambda b, p, i: (b * nq + i, q_col + p))
    k_spec = pl.BlockSpec((seq, LANES), lambda b, p, i: (b, k_col + p))
    vt_spec = pl.BlockSpec((LANES, seq), lambda b, p, i: (vt_row + p, b))
    if mode == "fox":
        in_specs = [q_spec, k_spec, pl.BlockSpec((seq, LANES), lambda b, p, i: (b, 0)), vt_spec]
        args = (q_arr, k_arr, aux_arr, vt_arr)
    else:
        in_specs = [q_spec, pl.BlockSpec((tq, LANES), lambda b, p, i: (b * nq + i, 0)), k_spec, vt_spec]
        args = (q_arr, aux_arr, k_arr, vt_arr)
    return pl.pallas_call(
        functools.partial(_attn_kernel, tq=tq, tk=tk, nblk=seq // tk, mode=mode),
        grid=(nb, pairs, nq),
        in_specs=in_specs,
        out_specs=pl.BlockSpec((tq, LANES), lambda b, p, i: (b * nq + i, p)),
        out_shape=jax.ShapeDtypeStruct((m, pairs * LANES), BF16),
        scratch_shapes=[pltpu.VMEM((2, seq // tk, HEAD_DIM + DENOM_ROWS, tk), BF16),
                        pltpu.VMEM((2, HEAD_DIM + DENOM_ROWS, tq), F32),
                        pltpu.VMEM((2, tk, tq), F32), pltpu.VMEM((2, tk, tq), BF16),
                        pltpu.VMEM((2, tq, 2 * LANES), BF16), pltpu.VMEM((2, 1, tq), F32)],
        compiler_params=_cparams(("arbitrary", "arbitrary", "arbitrary")),
        name=mode + "_attn",
    )(*args)


def _rope_tables(seq):
    half = ROPE_DIM // 2
    inv_freq = ROPE_THETA ** (-jnp.arange(half, dtype=F32) / half)
    ang = jnp.arange(seq, dtype=F32)[:, None] * inv_freq[None, :]
    cos, sin = jnp.cos(ang), jnp.sin(ang)
    ones = jnp.ones((seq, HEAD_DIM - ROPE_DIM), F32)
    cos_h = jnp.concatenate([cos, cos, ones], axis=1)
    sin_h = jnp.concatenate([-sin, sin, 0.0 * ones], axis=1)
    return jnp.tile(cos_h, (1, 2)), jnp.tile(sin_h, (1, 2))


def _rotary(x, cos_t, sin_t):
    half = ROPE_DIM // 2
    lane = lax.broadcasted_iota(jnp.int32, (x.shape[0], LANES), 1)
    first_half = (lane & (HEAD_DIM - 1)) < half
    outs = []
    for j in range(x.shape[1] // LANES):
        xc = x[:, j * LANES:(j + 1) * LANES]
        up = pltpu.roll(xc, LANES - half, axis=1)
        down = pltpu.roll(xc, half, axis=1)
        outs.append(xc * cos_t + jnp.where(first_half, up, down) * sin_t)
    return jnp.concatenate(outs, axis=1)


def _moba_k_kernel(k_ref, cos_ref, sin_ref, ko_ref, km_ref):
    kr = _rotary(k_ref[...], cos_ref[...], sin_ref[...])
    ko_ref[...] = kr.astype(ko_ref.dtype)
    km_ref[0] = jnp.mean(kr, axis=0, keepdims=True)


def _moba_k_prep(proj, k_col, cos_t, sin_t, seq):
    m = proj.shape[0]
    t = MOBA_BLOCK
    nblk = seq // t
    return pl.pallas_call(
        _moba_k_kernel,
        grid=(m // t,),
        in_specs=[pl.BlockSpec((t, MOBA_WIDTH), lambda i: (i, k_col)),
                  pl.BlockSpec((t, LANES), lambda i: (i % nblk, 0)),
                  pl.BlockSpec((t, LANES), lambda i: (i % nblk, 0))],
        out_specs=[pl.BlockSpec((t, MOBA_WIDTH), lambda i: (i, 0)),
                   pl.BlockSpec((1, 1, MOBA_WIDTH), lambda i: (i, 0, 0))],
        out_shape=[jax.ShapeDtypeStruct((m, MOBA_WIDTH), BF16),
                   jax.ShapeDtypeStruct((m // t, 1, MOBA_WIDTH), F32)],
        compiler_params=_cparams(("parallel",)),
        name="moba_k_prep",
    )(proj, cos_t, sin_t)


def _moba_q_kernel(q_ref, cos_ref, sin_ref, km_ref, qo_ref, ns_ref, *, nblk):
    own = pl.program_id(0) % nblk
    qr = _rotary(q_ref[...], cos_ref[...], sin_ref[...])
    qo_ref[...] = (qr * QK_SCALE).astype(qo_ref.dtype)
    gate_t = _dot_f32_nt(km_ref[0], qr)
    slots = LANES // MOBA_HEADS
    n = lax.broadcasted_iota(jnp.int32, (slots, gate_t.shape[1]), 0)
    flags = []
    for h in range(MOBA_HEADS):
        g = gate_t[h * slots:(h + 1) * slots, :]
        cnt = jnp.zeros(g.shape, jnp.int32)
        for cand in range(slots):
            other = g[cand:cand + 1, :]
            beats = (other > g) | ((other == g) & (n > cand))
            cnt = cnt + jnp.where(beats, jnp.where(cand < own, 1, 0), 0)
        selected = (n == own) | ((n < own) & (cnt < MOBA_TOPK))
        flags.append(jnp.where(selected, 0.0, 1.0))
    ns_ref[...] = jnp.concatenate(flags, axis=0).T.astype(ns_ref.dtype)


def _moba_q_prep(proj, q_col, cos_t, sin_t, km_mat, seq):
    m = proj.shape[0]
    t = MOBA_BLOCK
    nblk = seq // t
    return pl.pallas_call(
        functools.partial(_moba_q_kernel, nblk=nblk),
        grid=(m // t,),
        in_specs=[pl.BlockSpec((t, MOBA_WIDTH), lambda i: (i, q_col)),
                  pl.BlockSpec((t, LANES), lambda i: (i % nblk, 0)),
                  pl.BlockSpec((t, LANES), lambda i: (i % nblk, 0)),
                  pl.BlockSpec((1, LANES, MOBA_WIDTH), lambda i: (i // nblk, 0, 0))],
        out_specs=[pl.BlockSpec((t, MOBA_WIDTH), lambda i: (i, 0)),
                   pl.BlockSpec((t, LANES), lambda i: (i, 0))],
        out_shape=[jax.ShapeDtypeStruct((m, MOBA_WIDTH), BF16),
                   jax.ShapeDtypeStruct((m, LANES), BF16)],
        compiler_params=_cparams(("parallel",)),
        name="moba_q_prep",
    )(proj, cos_t, sin_t, km_mat)


def _pair_cols(x, h0, h1, shape):
    lane = lax.broadcasted_iota(jnp.int32, shape, 1)
    a = jnp.broadcast_to(x[:, h0:h0 + 1], shape)
    b = jnp.broadcast_to(x[:, h1:h1 + 1], shape)
    return jnp.where(lane < SSM_HEAD_DIM, a, b)


def _ssd_kernel(in_ref, cw_ref, cb_ref, dtb_ref, alog_ref, dsk_ref, nw_ref, o_ref, xpad_scr, st_scr):
    t = SSM_CHUNK
    c = pl.program_id(1)
    halo = SUBLANES

    @pl.when(c == 0)
    def _():
        xpad_scr[0:halo, :] = jnp.zeros((halo, SSM_CONV_DIM), F32)
        st_scr[...] = jnp.zeros(st_scr.shape, F32)

    xpad_scr[halo:halo + t, :] = in_ref[:, 0:SSM_CONV_DIM]
    conv = cb_ref[...] + cw_ref[0:1, :] * xpad_scr[pl.ds(halo - SSM_CONV + 1, t), :]
    for k in range(1, SSM_CONV):
        conv = conv + cw_ref[k:k + 1, :] * xpad_scr[pl.ds(halo - SSM_CONV + 1 + k, t), :]
    xpad_scr[0:halo, :] = xpad_scr[t:t + halo, :]
    xbc = conv * _sigmoid(conv)
    z = in_ref[:, SSM_CONV_DIM:SSM_CONV_DIM + SSM_WIDTH]
    dt = _softplus(in_ref[:, SSM_CONV_DIM + SSM_WIDTH:] + dtb_ref[...])
    a = -jnp.exp(alog_ref[...])
    acum = _tri_cumsum(dt * a)
    acum_t = acum.T
    tot = acum[t - 1:t, :]
    decay_end = jnp.exp(tot - acum)
    decay_in = jnp.exp(acum)
    exp_tot = jnp.exp(tot)
    row = lax.broadcasted_iota(jnp.int32, (t, t), 0)
    col = lax.broadcasted_iota(jnp.int32, (t, t), 1)
    causal = row >= col
    lane = lax.broadcasted_iota(jnp.int32, (t, LANES), 1)
    lane1 = lax.broadcasted_iota(jnp.int32, (1, LANES), 1)
    pair_shape = (t, LANES)
    rep = SSM_HEADS // SSM_GROUPS
    b_off = SSM_WIDTH
    c_off = SSM_WIDTH + SSM_GROUPS * SSM_STATE
    ys = []
    for g in range(SSM_GROUPS):
        bm = xbc[:, b_off + g * SSM_STATE:b_off + (g + 1) * SSM_STATE]
        cm = xbc[:, c_off + g * SSM_STATE:c_off + (g + 1) * SSM_STATE].astype(BF16)
        cb = _dot_nt(cm, bm.astype(BF16))
        bm_t = bm.T.astype(BF16)
        for pr in range(rep // 2):
            j = g * (rep // 2) + pr
            h0, h1 = 2 * j, 2 * j + 1
            xp = xbc[:, j * LANES:(j + 1) * LANES]
            xdt = xp * _pair_cols(dt, h0, h1, pair_shape)
            xdt_b = xdt.astype(BF16)
            yd = []
            for h in (h0, h1):
                diff = acum[:, h:h + 1] - acum_t[h:h + 1, :]
                w = cb * jnp.exp(jnp.where(causal, diff, NEG_BIG))
                yd.append(_dot(w.astype(BF16), xdt_b))
            y_diag = jnp.where(lane < SSM_HEAD_DIM, yd[0], yd[1])
            prev = st_scr[j]
            y_off = _dot(cm, prev.astype(BF16)) * _pair_cols(decay_in, h0, h1, pair_shape)
            xw = (xdt * _pair_cols(decay_end, h0, h1, pair_shape)).astype(BF16)
            scale = jnp.where(lane1 < SSM_HEAD_DIM, exp_tot[:, h0:h0 + 1], exp_tot[:, h1:h1 + 1])
            st_scr[j] = prev * scale + _dot(bm_t, xw)
            ys.append(y_diag + y_off + dsk_ref[:, j * LANES:(j + 1) * LANES] * xp)
    y = jnp.concatenate(ys, axis=1)
    y = y * (z * _sigmoid(z))
    gw = SSM_WIDTH // SSM_GROUPS
    outs = []
    for g in range(SSM_GROUPS):
        seg = y[:, g * gw:(g + 1) * gw]
        ms = jnp.mean(seg * seg, axis=-1, keepdims=True)
        outs.append(seg * lax.rsqrt(ms + NORM_EPS) * nw_ref[:, g * gw:(g + 1) * gw])
    o_ref[...] = jnp.concatenate(outs, axis=1).astype(o_ref.dtype)


def _ssd(proj, conv_w, conv_b, dt_bias, a_log, d_skip, norm_w, seq):
    m, wtot = proj.shape
    t = SSM_CHUNK
    nc = seq // t
    pad = LANES - SSM_HEADS
    dtb = jnp.pad(dt_bias, (0, pad)).reshape(1, LANES)
    alog = jnp.pad(a_log, (0, pad)).reshape(1, LANES)
    dsk = jnp.repeat(d_skip, SSM_HEAD_DIM).reshape(1, SSM_WIDTH)
    const = lambda b, c: (0, 0)
    return pl.pallas_call(
        _ssd_kernel,
        grid=(m // seq, nc),
        in_specs=[pl.BlockSpec((t, wtot), lambda b, c: (b * nc + c, 0)),
                  pl.BlockSpec((SSM_CONV, SSM_CONV_DIM), const),
                  pl.BlockSpec((1, SSM_CONV_DIM), const),
                  pl.BlockSpec((1, LANES), const),
                  pl.BlockSpec((1, LANES), const),
                  pl.BlockSpec((1, SSM_WIDTH), const),
                  pl.BlockSpec((1, SSM_WIDTH), const)],
        out_specs=pl.BlockSpec((t, SSM_WIDTH), lambda b, c: (b * nc + c, 0)),
        out_shape=jax.ShapeDtypeStruct((m, SSM_WIDTH), BF16),
        scratch_shapes=[pltpu.VMEM((t + SUBLANES, SSM_CONV_DIM), F32),
                        pltpu.VMEM((SSM_HEADS // 2, SSM_STATE, LANES), F32)],
        compiler_params=_cparams(("parallel", "arbitrary")),
        name="ssd",
    )(proj, conv_w, conv_b.reshape(1, SSM_CONV_DIM), dtb, alog, dsk, norm_w.reshape(1, SSM_WIDTH))


def _merge_kernel(ya_ref, yb_ref, yc_ref, yd_ref, g_ref, x_ref, g1_ref,
                  wa_ref, wb_ref, wc_ref, wd_ref, wo_ref, o_ref):
    d = x_ref.shape[1]
    gate = lambda i: _sigmoid(g_ref[:, i * d:(i + 1) * d].astype(F32))
    merged = gate(0) * _dot(ya_ref[...], wa_ref[...])
    merged = merged + gate(1) * _dot(yb_ref[...], wb_ref[...])
    merged = merged + gate(2) * _dot(yc_ref[...], wc_ref[...])
    merged = merged + gate(3) * _dot(yd_ref[...], wd_ref[...])
    o_ref[...] = x_ref[...] + g1_ref[0] * _dot(merged.astype(BF16), wo_ref[...])


def _merge(ya, yb, yc, yd, gates, x, mod, k_gate, wa, wb, wc, wd, wo, seq, tm):
    m, d = x.shape
    nb = m // seq
    rows = lambda i: (i, 0)
    const = lambda i: (0, 0)
    return pl.pallas_call(
        _merge_kernel,
        grid=(m // tm,),
        in_specs=[pl.BlockSpec((tm, ya.shape[1]), rows),
                  pl.BlockSpec((tm, yb.shape[1]), rows),
                  pl.BlockSpec((tm, yc.shape[1]), rows),
                  pl.BlockSpec((tm, yd.shape[1]), rows),
                  pl.BlockSpec((tm, N_BRANCH * d), rows),
                  pl.BlockSpec((tm, d), rows),
                  pl.BlockSpec((1, 1, d), lambda i: (k_gate * nb + (i * tm) // seq, 0, 0)),
                  pl.BlockSpec(wa.shape, const),
                  pl.BlockSpec(wb.shape, const),
                  pl.BlockSpec(wc.shape, const),
                  pl.BlockSpec(wd.shape, const),
                  pl.BlockSpec(wo.shape, const)],
        out_specs=pl.BlockSpec((tm, d), rows),
        out_shape=jax.ShapeDtypeStruct((m, d), F32),
        compiler_params=_cparams(("parallel",)),
        name="merge",
    )(ya, yb, yc, yd, gates, x, mod, wa, wb, wc, wd, wo)


def _ffn_kernel(x_ref, nw_ref, sc_ref, sh_ref, g_ref, w1_ref, w2_ref, o_ref, h_scr, acc_scr):
    j = pl.program_id(1)

    @pl.when(j == 0)
    def _():
        h_scr[...] = _rms_mod(x_ref[...], nw_ref[...], sc_ref[0], sh_ref[0]).astype(BF16)
        acc_scr[...] = jnp.zeros(acc_scr.shape, F32)

    a = jnp.maximum(_dot(h_scr[...], w1_ref[...]), 0.0)
    acc_scr[...] += _dot((a * a).astype(BF16), w2_ref[...])

    @pl.when(j == pl.num_programs(1) - 1)
    def _():
        o_ref[...] = x_ref[...] + g_ref[0] * acc_scr[...]


def _ffn(x, nw, mod, k_sc, k_sh, k_gate, w1, w2, seq, tm, tf):
    m, d = x.shape
    nb = m // seq
    dff = w1.shape[1]
    modrow = lambda k: (lambda i, j: (k * nb + (i * tm) // seq, 0, 0))
    return pl.pallas_call(
        _ffn_kernel,
        grid=(m // tm, dff // tf),
        in_specs=[pl.BlockSpec((tm, d), lambda i, j: (i, 0)),
                  pl.BlockSpec((1, d), lambda i, j: (0, 0)),
                  pl.BlockSpec((1, 1, d), modrow(k_sc)),
                  pl.BlockSpec((1, 1, d), modrow(k_sh)),
                  pl.BlockSpec((1, 1, d), modrow(k_gate)),
                  pl.BlockSpec((d, tf), lambda i, j: (0, j)),
                  pl.BlockSpec((tf, d), lambda i, j: (j, 0))],
        out_specs=pl.BlockSpec((tm, d), lambda i, j: (i, 0)),
        out_shape=jax.ShapeDtypeStruct((m, d), F32),
        scratch_shapes=[pltpu.VMEM((tm, d), BF16), pltpu.VMEM((tm, d), F32)],
        compiler_params=_cparams(("parallel", "arbitrary")),
        name="ffn",
    )(x, nw.reshape(1, d), mod, mod, mod, w1, w2)


def _pad_rows(w, rows):
    return jnp.pad(w, ((0, rows - w.shape[0]), (0, 0)))


def _split_w_in(w_in, l):
    wt = jnp.transpose(w_in, (2, 0, 1))[:, l, :]
    o = 0
    uv = wt[o:o + 2 * GMLP_WIDTH]; o += 2 * GMLP_WIDTH
    fox_qk = wt[o:o + 2 * FOX_WIDTH]; o += 2 * FOX_WIDTH
    fox_v = wt[o:o + FOX_WIDTH]; o += FOX_WIDTH
    fox_f = wt[o:o + FOX_HEADS]; o += FOX_HEADS
    moba_qk = wt[o:o + 2 * MOBA_WIDTH]; o += 2 * MOBA_WIDTH
    moba_v = wt[o:o + MOBA_WIDTH]; o += MOBA_WIDTH
    z = wt[o:o + SSM_WIDTH]; o += SSM_WIDTH
    xbc = wt[o:o + SSM_CONV_DIM]; o += SSM_CONV_DIM
    dt = wt[o:o + SSM_HEADS]; o += SSM_HEADS
    gates = wt[o:]
    w_v = jnp.concatenate([fox_v, moba_v], axis=0)
    w_mqkf = jnp.concatenate([moba_qk, _pad_rows(fox_f, LANES)], axis=0)
    w_ssd = jnp.concatenate([xbc, z, _pad_rows(dt, LANES)], axis=0)
    return [g.astype(BF16) for g in (uv, fox_qk, w_v, w_mqkf, w_ssd, gates)]


def _tile_rows(seq, want):
    return min(want, seq)


def kernel(x, c, ada_w, ada_b, norm_mix_w, w_in, gmlp_ln_w, gmlp_ln_b, gmlp_ws, gmlp_bs, fox_f_bias, ssm_conv_w, ssm_conv_b, ssm_dt_bias, ssm_a_log, ssm_d, ssm_norm_w, w_branch_a, w_branch_b, w_branch_c, w_branch_d, w_out, norm_mlp_w, mlp_w1, mlp_w2, final_norm_w):
    nb, seq, d = x.shape
    m = nb * seq
    depth = ada_w.shape[0]
    tm = _tile_rows(seq, 512)
    tm_proj = _tile_rows(seq, 1024)
    tq_attn = _tile_rows(seq, ATTN_Q_TILE)
    pairs = FOX_HEADS // 2
    nblk = seq // MOBA_BLOCK
    slots = LANES // MOBA_HEADS
    assert seq % MOBA_BLOCK == 0 and nblk <= slots and seq % SSM_CHUNK == 0

    mod_all = _modulation(c, ada_w, ada_b)[:, :nb]
    mod_all = mod_all.reshape(depth, nb, 6, d).transpose(0, 2, 1, 3).reshape(depth, 6 * nb, 1, d)
    cos_t, sin_t = _rope_tables(seq)
    head_eye = jnp.eye(MOBA_HEADS, dtype=F32)

    xf = x.reshape(m, d)
    for l in range(depth):
        mod = mod_all[l]
        w_uv, w_fqk, w_v, w_mqkf, w_ssd, w_gate = _split_w_in(w_in, l)
        h = _norm_mod(xf, norm_mix_w[l], mod, 1, 0, seq, tm)
        p_uv = _matmul(h, w_uv, F32, tm_proj, w_uv.shape[0], "proj_uv")
        p_fqk = _matmul(h, w_fqk, BF16, tm_proj, w_fqk.shape[0], "proj_fox_qk", FOX_WIDTH, QK_SCALE)
        p_vt = _matmul_t(h, w_v, BF16, tm_proj, "proj_vt")
        p_mqkf = _matmul(h, w_mqkf, F32, tm_proj, w_mqkf.shape[0], "proj_mqkf")
        p_ssd = _matmul(h, w_ssd, F32, tm_proj, w_ssd.shape[0], "proj_ssd")
        p_gate = _matmul(h, w_gate, BF16, tm_proj, 2 * d, "proj_gate")

        y_a = _gmlp(p_uv, gmlp_ln_w[l], gmlp_ln_b[l], gmlp_ws[l], gmlp_bs[l], tm)

        k_aux = _forget_cumsum(p_mqkf, 2 * MOBA_WIDTH // LANES, fox_f_bias[l], seq)
        y_b = _block_attention("fox", p_fqk, 0, k_aux, p_fqk, pairs, p_vt, 0, seq, tq_attn, MOBA_BLOCK)

        k_rot, k_mean = _moba_k_prep(p_mqkf, 1, cos_t, sin_t, seq)
        km = jnp.pad(k_mean.reshape(nb, nblk, MOBA_HEADS, HEAD_DIM), ((0, 0), (0, slots - nblk), (0, 0), (0, 0)))
        km_mat = jnp.einsum('bnhd,kh->bknhd', km, head_eye).reshape(nb, LANES, MOBA_WIDTH)
        q_rot, not_sel = _moba_q_prep(p_mqkf, 0, cos_t, sin_t, km_mat, seq)
        y_c = _block_attention("moba", q_rot, 0, not_sel, k_rot, 0, p_vt, pairs, seq, tq_attn, MOBA_BLOCK)

        y_d = _ssd(p_ssd, ssm_conv_w[l], ssm_conv_b[l], ssm_dt_bias[l], ssm_a_log[l], ssm_d[l],
                   ssm_norm_w[l], seq)

        xf = _merge(y_a, y_b, y_c, y_d, p_gate, xf, mod, 2,
                    w_branch_a[l].astype(BF16), w_branch_b[l].astype(BF16), w_branch_c[l].astype(BF16),
                    w_branch_d[l].astype(BF16), w_out[l].astype(BF16), seq, tm)
        xf = _ffn(xf, norm_mlp_w[l], mod, 4, 3, 5, mlp_w1[l].astype(BF16), mlp_w2[l].astype(BF16),
                  seq, _tile_rows(seq, 1024), 1024)
    return _final_norm(xf, final_norm_w, tm).reshape(nb, seq, d)
```

```python
import functools
import math

import jax
import jax.numpy as jnp
from jax import lax
from jax.experimental import pallas as pl
from jax.experimental.pallas import tpu as pltpu

F32 = jnp.float32
BF16 = jnp.bfloat16

HEAD_DIM = 64
NORM_EPS = 1e-6
GMLP_GROUPS = 8
GMLP_WIDTH = GMLP_GROUPS * HEAD_DIM
GMLP_CHUNK = 128
FOX_HEADS = 8
FOX_WIDTH = FOX_HEADS * HEAD_DIM
MOBA_HEADS = 8
MOBA_WIDTH = MOBA_HEADS * HEAD_DIM
MOBA_BLOCK = 256
MOBA_TOPK = 3
ROPE_THETA = 500000.0
ROPE_DIM = HEAD_DIM // 4
SSM_HEADS = 12
SSM_HEAD_DIM = 64
SSM_WIDTH = SSM_HEADS * SSM_HEAD_DIM
SSM_GROUPS = 2
SSM_STATE = 128
SSM_CONV = 4
SSM_CHUNK = 128
SSM_CONV_DIM = SSM_WIDTH + 2 * SSM_GROUPS * SSM_STATE
N_BRANCH = 4

LANES = 128
SUBLANES = 8
NEG_BIG = -1e30
VMEM_LIMIT = 48 * 1024 * 1024
ATTN_Q_TILE = 512


def _cparams(sem):
    return pltpu.CompilerParams(dimension_semantics=sem, vmem_limit_bytes=VMEM_LIMIT)


def _sigmoid(x):
    return 1.0 / (1.0 + jnp.exp(-x))


def _softplus(x):
    return jnp.maximum(x, 0.0) + jnp.log(1.0 + jnp.exp(-jnp.abs(x)))


def _dot(a, b):
    return jnp.dot(a, b, preferred_element_type=F32)


def _dot_nt(a, b):
    return lax.dot_general(a, b, (((1,), (1,)), ((), ())), preferred_element_type=F32)


def _keep_high_half(x):
    bits = lax.bitcast_convert_type(x, jnp.uint32) & jnp.uint32(0xFFFF0000)
    return lax.bitcast_convert_type(bits, F32)


def _split3(x):
    hi = _keep_high_half(x)
    r = x - hi
    mid = _keep_high_half(r)
    lo = r - mid
    return hi.astype(BF16), mid.astype(BF16), lo.astype(BF16)


def _tri_cumsum(x):
    t = x.shape[0]
    row = lax.broadcasted_iota(jnp.int32, (t, t), 0)
    col = lax.broadcasted_iota(jnp.int32, (t, t), 1)
    tri = jnp.where(row >= col, 1.0, 0.0).astype(BF16)
    hi, mid, lo = _split3(x)
    return _dot(tri, hi) + _dot(tri, mid) + _dot(tri, lo)


def _six_pass(dot, a, b):
    ah, am, al = _split3(a)
    bh, bm, bl = _split3(b)
    small = dot(am, bm) + dot(ah, bl) + dot(al, bh)
    return dot(ah, bh) + (dot(ah, bm) + dot(am, bh)) + small


def _dot_f32(a, b):
    return _six_pass(_dot, a, b)


def _dot_f32_nt(a, b):
    return _six_pass(_dot_nt, a, b)


def _mod_kernel(c_ref, w_ref, b_ref, o_ref):
    c = c_ref[...]
    ca = c * _sigmoid(c)
    o_ref[0] = _dot_f32(ca, w_ref[0]) + b_ref[0]


def _modulation(c, ada_w, ada_b):
    nb, d = c.shape
    nl = ada_w.shape[0]
    bp = -(-nb // SUBLANES) * SUBLANES
    cp = jnp.pad(c, ((0, bp - nb), (0, 0)))
    return pl.pallas_call(
        _mod_kernel,
        grid=(nl, 6),
        in_specs=[pl.BlockSpec((bp, d), lambda l, k: (0, 0)),
                  pl.BlockSpec((1, d, d), lambda l, k: (l, 0, k)),
                  pl.BlockSpec((1, 1, d), lambda l, k: (l, 0, k))],
        out_specs=pl.BlockSpec((1, bp, d), lambda l, k: (l, 0, k)),
        out_shape=jax.ShapeDtypeStruct((nl, bp, 6 * d), F32),
        compiler_params=_cparams(("parallel", "parallel")),
        name="adaln_mod",
    )(cp, ada_w, ada_b.reshape(nl, 1, 6 * d))


def _rms_mod(x, nw, sc, sh):
    ms = jnp.mean(x * x, axis=-1, keepdims=True)
    xn = x * lax.rsqrt(ms + NORM_EPS)
    return xn * nw * (1.0 + sc) + sh


def _norm_kernel(x_ref, nw_ref, sc_ref, sh_ref, o_ref):
    o_ref[...] = _rms_mod(x_ref[...], nw_ref[...], sc_ref[0], sh_ref[0]).astype(o_ref.dtype)


def _norm_mod(x, nw, mod, k_sc, k_sh, seq, tm):
    m, d = x.shape
    nb = m // seq
    return pl.pallas_call(
        _norm_kernel,
        grid=(m // tm,),
        in_specs=[pl.BlockSpec((tm, d), lambda i: (i, 0)),
                  pl.BlockSpec((1, d), lambda i: (0, 0)),
                  pl.BlockSpec((1, 1, d), lambda i: (k_sc * nb + (i * tm) // seq, 0, 0)),
                  pl.BlockSpec((1, 1, d), lambda i: (k_sh * nb + (i * tm) // seq, 0, 0))],
        out_specs=pl.BlockSpec((tm, d), lambda i: (i, 0)),
        out_shape=jax.ShapeDtypeStruct((m, d), BF16),
        compiler_params=_cparams(("parallel",)),
        name="norm_mod",
    )(x, nw.reshape(1, d), mod, mod)


def _final_norm_kernel(x_ref, nw_ref, o_ref):
    x = x_ref[...]
    ms = jnp.mean(x * x, axis=-1, keepdims=True)
    o_ref[...] = x * lax.rsqrt(ms + NORM_EPS) * nw_ref[...]


def _final_norm(x, nw, tm):
    m, d = x.shape
    return pl.pallas_call(
        _final_norm_kernel,
        grid=(m // tm,),
        in_specs=[pl.BlockSpec((tm, d), lambda i: (i, 0)),
                  pl.BlockSpec((1, d), lambda i: (0, 0))],
        out_specs=pl.BlockSpec((tm, d), lambda i: (i, 0)),
        out_shape=jax.ShapeDtypeStruct((m, d), F32),
        compiler_params=_cparams(("parallel",)),
        name="final_norm",
    )(x, nw.reshape(1, d))


def _mm_kernel(a_ref, wt_ref, o_ref):
    o_ref[...] = _dot_nt(a_ref[...], wt_ref[...]).astype(o_ref.dtype)


def _matmul(a, wt, out_dtype, tm, tn, name):
    m, k = a.shape
    n = wt.shape[0]
    return pl.pallas_call(
        _mm_kernel,
        grid=(n // tn, m // tm),
        in_specs=[pl.BlockSpec((tm, k), lambda j, i: (i, 0)),
                  pl.BlockSpec((tn, k), lambda j, i: (j, 0))],
        out_specs=pl.BlockSpec((tm, tn), lambda j, i: (i, j)),
        out_shape=jax.ShapeDtypeStruct((m, n), out_dtype),
        compiler_params=_cparams(("parallel", "parallel")),
        name=name,
    )(a, wt)


def _mm_t_kernel(a_ref, wt_ref, o_ref):
    o_ref[...] = _dot_nt(wt_ref[...], a_ref[...]).astype(o_ref.dtype)


def _matmul_t(a, wt, out_dtype, tm, name):
    m, k = a.shape
    n = wt.shape[0]
    return pl.pallas_call(
        _mm_t_kernel,
        grid=(m // tm,),
        in_specs=[pl.BlockSpec((tm, k), lambda i: (i, 0)),
                  pl.BlockSpec((n, k), lambda i: (0, 0))],
        out_specs=pl.BlockSpec((n, tm), lambda i: (0, i)),
        out_shape=jax.ShapeDtypeStruct((n, m), out_dtype),
        compiler_params=_cparams(("parallel",)),
        name=name,
    )(a, wt)


def _gelu_tanh(x):
    c = math.sqrt(2.0 / math.pi)
    return 0.5 * x * (1.0 + jnp.tanh(c * (x + 0.044715 * (x * x * x))))


def _gmlp_kernel(uv_ref, lnw_ref, lnb_ref, ws_ref, bs_ref, o_ref, *, tm):
    g = _gelu_tanh(uv_ref[...])
    u = g[:, :GMLP_WIDTH]
    v = g[:, GMLP_WIDTH:]
    mu = jnp.mean(v, axis=-1, keepdims=True)
    vc = v - mu
    var = jnp.mean(vc * vc, axis=-1, keepdims=True)
    vn = (vc * lax.rsqrt(var + NORM_EPS) * lnw_ref[...] + lnb_ref[...]).astype(BF16)
    t = GMLP_CHUNK
    row = lax.broadcasted_iota(jnp.int32, (t, t), 0)
    col = lax.broadcasted_iota(jnp.int32, (t, t), 1)
    causal = row >= col
    ws = [jnp.where(causal, ws_ref[i], 0.0).astype(BF16) for i in range(GMLP_GROUPS)]
    first_head = lax.broadcasted_iota(jnp.int32, (t, LANES), 1) < HEAD_DIM
    for c in range(tm // t):
        rows = slice(c * t, (c + 1) * t)
        for p in range(GMLP_WIDTH // LANES):
            cols = slice(p * LANES, (p + 1) * LANES)
            vp = vn[rows, cols]
            mixed = jnp.where(first_head, _dot(ws[2 * p], vp), _dot(ws[2 * p + 1], vp))
            o_ref[rows, cols] = (u[rows, cols] * (mixed + bs_ref[:, cols])).astype(o_ref.dtype)


def _gmlp(uv, ln_w, ln_b, ws, bs, tm):
    m = uv.shape[0]
    w = GMLP_WIDTH
    bs_full = jnp.repeat(bs.T, HEAD_DIM, axis=1)
    return pl.pallas_call(
        functools.partial(_gmlp_kernel, tm=tm),
        grid=(m // tm,),
        in_specs=[pl.BlockSpec((tm, 2 * w), lambda i: (i, 0)),
                  pl.BlockSpec((1, w), lambda i: (0, 0)),
                  pl.BlockSpec((1, w), lambda i: (0, 0)),
                  pl.BlockSpec((GMLP_GROUPS, GMLP_CHUNK, GMLP_CHUNK), lambda i: (0, 0, 0)),
                  pl.BlockSpec((GMLP_CHUNK, w), lambda i: (0, 0))],
        out_specs=pl.BlockSpec((tm, w), lambda i: (i, 0)),
        out_shape=jax.ShapeDtypeStruct((m, w), BF16),
        compiler_params=_cparams(("parallel",)),
        name="gmlp",
    )(uv, ln_w.reshape(1, w), ln_b.reshape(1, w), ws, bs_full)


def _fcum_kernel(f_ref, b_ref, o_ref, *, seq):
    t = LANES
    r = lax.broadcasted_iota(jnp.int32, (LANES, LANES), 0)
    c = lax.broadcasted_iota(jnp.int32, (LANES, LANES), 1)
    spread = [jnp.where((c == 3 * r + part) & (r < FOX_HEADS), 1.0, 0.0).astype(BF16) for part in range(3)]

    def body(i, carry):
        r0 = pl.multiple_of(i * t, t)
        z = f_ref[pl.ds(r0, t), :] + b_ref[...]
        logf = jnp.minimum(z, 0.0) - jnp.log(1.0 + jnp.exp(-jnp.abs(z)))
        cs = _tri_cumsum(logf) + carry
        hi, mid, lo = _split3(-cs)
        o_ref[pl.ds(r0, t), :] = (_dot(hi, spread[0]) + _dot(mid, spread[1]) + _dot(lo, spread[2])).astype(o_ref.dtype)
        return cs[t - 1:t, :]

    lax.fori_loop(0, seq // t, body, jnp.zeros((1, LANES), F32))


def _forget_cumsum(proj, col_block, f_bias, seq):
    m = proj.shape[0]
    fb = jnp.pad(f_bias, (0, LANES - FOX_HEADS)).reshape(1, LANES)
    return pl.pallas_call(
        functools.partial(_fcum_kernel, seq=seq),
        grid=(m // seq,),
        in_specs=[pl.BlockSpec((seq, LANES), lambda b: (b, col_block)),
                  pl.BlockSpec((1, LANES), lambda b: (0, 0))],
        out_specs=pl.BlockSpec((seq, LANES), lambda b: (b, 0)),
        out_shape=jax.ShapeDtypeStruct((m, LANES), BF16),
        compiler_params=_cparams(("parallel",)),
        name="fox_cumsum",
    )(proj, fb)


def _attn_kernel(*refs, tq, tk, nblk, mode):
    if mode == "fox":
        q_ref, k_ref, ka_ref, vt_ref, o_ref, vp_scr, acc_scr, st_scr, p_scr, qc_scr = refs
    else:
        q_ref, ns_ref, k_ref, vt_ref, o_ref, vp_scr, acc_scr, st_scr, p_scr, qc_scr = refs
    p = pl.program_id(1)
    qi = pl.program_id(2)
    half = LANES // 2
    n_sub = tq // tk

    @pl.when(qi == 0)
    def _():
        chan = lax.broadcasted_iota(jnp.int32, (LANES, tk), 0)
        for jb in range(nblk):
            vt = vt_ref[:, jb * tk:(jb + 1) * tk]
            ones = jnp.ones_like(vt)
            vp_scr[0, jb] = jnp.where(chan < half, vt, ones)
            vp_scr[1, jb] = jnp.where(chan >= half, vt, ones)

    lane_q = lax.broadcasted_iota(jnp.int32, (tq, LANES), 1)
    lane_k = lax.broadcasted_iota(jnp.int32, (tk, LANES), 1)
    q = q_ref[...]
    if mode == "fox":
        q = q * (HEAD_DIM ** -0.5)
    slots = LANES // MOBA_HEADS
    for h in range(2):
        head_lanes = (lane_q < half) if h == 0 else (lane_q >= half)
        qh = jnp.where(head_lanes, q, jnp.zeros_like(q))
        if mode == "fox":
            first = 3 * (2 * p + h)
            aux = jnp.where((lane_q >= first) & (lane_q < first + 3), 1.0, 0.0).astype(BF16)
        else:
            aux = ns_ref[...]
        qc_scr[h] = jnp.concatenate([qh, aux], axis=1)

    def scores(j):
        r0 = pl.multiple_of(j * tk, tk)
        kb = k_ref[pl.ds(r0, tk), :]
        out = []
        for h in range(2):
            if mode == "fox":
                ka = ka_ref[pl.ds(r0, tk), :]
            else:
                ka = jnp.where(lane_k == (2 * p + h) * slots + j, NEG_BIG, 0.0).astype(BF16)
            out.append(_dot_nt(jnp.concatenate([kb, ka], axis=1), qc_scr[h]))
        return out

    def advance(j, ms, mask, prefetch):
        nxt = scores(j + 1) if prefetch else None
        prev = jnp.maximum(j - 1, 0)
        pv = [_dot(vp_scr[h, prev], p_scr[h]) for h in range(2)]
        ms2 = []
        for h in range(2):
            st = st_scr[h]
            if mask is not None:
                st = jnp.where(mask, st, NEG_BIG)
            m_new = jnp.maximum(ms[h], jnp.max(st, axis=0, keepdims=True))
            alpha = jnp.exp(ms[h] - m_new)
            pt = jnp.exp(st - m_new).astype(BF16)
            acc_scr[h] = alpha * (acc_scr[h] + pv[h])
            p_scr[h] = pt
            ms2.append(m_new)
        if prefetch:
            for h in range(2):
                st_scr[h] = nxt[h]
        return tuple(ms2)

    acc_scr[...] = jnp.zeros(acc_scr.shape, F32)
    p_scr[...] = jnp.zeros(p_scr.shape, BF16)
    first_scores = scores(0)
    for h in range(2):
        st_scr[h] = first_scores[h]
    m0 = jnp.full((1, tq), -jnp.inf, F32)
    first_diag = qi * n_sub

    def body(i, ms):
        for s in range(n_sub):
            ms = advance(i * n_sub + s, ms, None, True)
        return ms

    ms = lax.fori_loop(0, qi, body, (m0, m0))
    key_i = lax.broadcasted_iota(jnp.int32, (tk, tq), 0)
    qry_i = lax.broadcasted_iota(jnp.int32, (tk, tq), 1)
    for s in range(n_sub):
        ms = advance(first_diag + s, ms, (key_i + s * tk) <= qry_i, s + 1 < n_sub)
    last = first_diag + n_sub - 1
    outs = []
    for h in range(2):
        a = acc_scr[h] + _dot(vp_scr[h, last], p_scr[h])
        denom = a[(1 - h) * half:(1 - h) * half + 1, :]
        outs.append(a[h * half:(h + 1) * half, :] / denom)
    o_ref[...] = jnp.concatenate(outs, axis=0).T.astype(o_ref.dtype)


def _block_attention(mode, q_arr, q_col, aux_arr, k_arr, k_col, vt_arr, vt_row, seq, tq, tk):
    m = q_arr.shape[0]
    nb = m // seq
    nq = seq // tq
    pairs = FOX_HEADS // 2
    q_spec = pl.BlockSpec((tq, LANES), lambda b, p, i: (b * nq + i, q_col + p))
    k_spec = pl.BlockSpec((seq, LANES), lambda b, p, i: (b, k_col + p))
    vt_spec = pl.BlockSpec((LANES, seq), lambda b, p, i: (vt_row + p, b))
    if mode == "fox":
        in_specs = [q_spec, k_spec, pl.BlockSpec((seq, LANES), lambda b, p, i: (b, 0)), vt_spec]
        args = (q_arr, k_arr, aux_arr, vt_arr)
    else:
        in_specs = [q_spec, pl.BlockSpec((tq, LANES), lambda b, p, i: (b * nq + i, 0)), k_spec, vt_spec]
        args = (q_arr, aux_arr, k_arr, vt_arr)
    return pl.pallas_call(
        functools.partial(_attn_kernel, tq=tq, tk=tk, nblk=seq // tk, mode=mode),
        grid=(nb, pairs, nq),
        in_specs=in_specs,
        out_specs=pl.BlockSpec((tq, LANES), lambda b, p, i: (b * nq + i, p)),
        out_shape=jax.ShapeDtypeStruct((m, pairs * LANES), BF16),
        scratch_shapes=[pltpu.VMEM((2, seq // tk, LANES, tk), BF16), pltpu.VMEM((2, LANES, tq), F32),
                        pltpu.VMEM((2, tk, tq), F32), pltpu.VMEM((2, tk, tq), BF16),
                        pltpu.VMEM((2, tq, 2 * LANES), BF16)],
        compiler_params=_cparams(("arbitrary", "arbitrary", "arbitrary")),
        name=mode + "_attn",
    )(*args)


def _rope_tables(seq):
    half = ROPE_DIM // 2
    inv_freq = ROPE_THETA ** (-jnp.arange(half, dtype=F32) / half)
    ang = jnp.arange(seq, dtype=F32)[:, None] * inv_freq[None, :]
    cos, sin = jnp.cos(ang), jnp.sin(ang)
    ones = jnp.ones((seq, HEAD_DIM - ROPE_DIM), F32)
    cos_h = jnp.concatenate([cos, cos, ones], axis=1)
    sin_h = jnp.concatenate([-sin, sin, 0.0 * ones], axis=1)
    return jnp.tile(cos_h, (1, 2)), jnp.tile(sin_h, (1, 2))


def _rotary(x, cos_t, sin_t):
    half = ROPE_DIM // 2
    lane = lax.broadcasted_iota(jnp.int32, (x.shape[0], LANES), 1)
    first_half = (lane & (HEAD_DIM - 1)) < half
    outs = []
    for j in range(x.shape[1] // LANES):
        xc = x[:, j * LANES:(j + 1) * LANES]
        up = pltpu.roll(xc, LANES - half, axis=1)
        down = pltpu.roll(xc, half, axis=1)
        outs.append(xc * cos_t + jnp.where(first_half, up, down) * sin_t)
    return jnp.concatenate(outs, axis=1)


def _moba_k_kernel(k_ref, cos_ref, sin_ref, ko_ref, km_ref):
    kr = _rotary(k_ref[...], cos_ref[...], sin_ref[...])
    ko_ref[...] = kr.astype(ko_ref.dtype)
    km_ref[0] = jnp.mean(kr, axis=0, keepdims=True)


def _moba_k_prep(proj, k_col, cos_t, sin_t, seq):
    m = proj.shape[0]
    t = MOBA_BLOCK
    nblk = seq // t
    return pl.pallas_call(
        _moba_k_kernel,
        grid=(m // t,),
        in_specs=[pl.BlockSpec((t, MOBA_WIDTH), lambda i: (i, k_col)),
                  pl.BlockSpec((t, LANES), lambda i: (i % nblk, 0)),
                  pl.BlockSpec((t, LANES), lambda i: (i % nblk, 0))],
        out_specs=[pl.BlockSpec((t, MOBA_WIDTH), lambda i: (i, 0)),
                   pl.BlockSpec((1, 1, MOBA_WIDTH), lambda i: (i, 0, 0))],
        out_shape=[jax.ShapeDtypeStruct((m, MOBA_WIDTH), BF16),
                   jax.ShapeDtypeStruct((m // t, 1, MOBA_WIDTH), F32)],
        compiler_params=_cparams(("parallel",)),
        name="moba_k_prep",
    )(proj, cos_t, sin_t)


def _moba_q_kernel(q_ref, cos_ref, sin_ref, km_ref, qo_ref, ns_ref, *, nblk):
    own = pl.program_id(0) % nblk
    qr = _rotary(q_ref[...], cos_ref[...], sin_ref[...])
    qo_ref[...] = (qr * (HEAD_DIM ** -0.5)).astype(qo_ref.dtype)
    gate_t = _dot_f32_nt(km_ref[0], qr)
    slots = LANES // MOBA_HEADS
    n = lax.broadcasted_iota(jnp.int32, (slots, gate_t.shape[1]), 0)
    flags = []
    for h in range(MOBA_HEADS):
        g = gate_t[h * slots:(h + 1) * slots, :]
        cnt = jnp.zeros(g.shape, jnp.int32)
        for cand in range(slots):
            other = g[cand:cand + 1, :]
            beats = (other > g) | ((other == g) & (n > cand))
            cnt = cnt + jnp.where(beats, jnp.where(cand < own, 1, 0), 0)
        selected = (n == own) | ((n < own) & (cnt < MOBA_TOPK))
        flags.append(jnp.where(selected, 0.0, 1.0))
    ns_ref[...] = jnp.concatenate(flags, axis=0).T.astype(ns_ref.dtype)


def _moba_q_prep(proj, q_col, cos_t, sin_t, km_mat, seq):
    m = proj.shape[0]
    t = MOBA_BLOCK
    nblk = seq // t
    return pl.pallas_call(
        functools.partial(_moba_q_kernel, nblk=nblk),
        grid=(m // t,),
        in_specs=[pl.BlockSpec((t, MOBA_WIDTH), lambda i: (i, q_col)),
                  pl.BlockSpec((t, LANES), lambda i: (i % nblk, 0)),
                  pl.BlockSpec((t, LANES), lambda i: (i % nblk, 0)),
                  pl.BlockSpec((1, LANES, MOBA_WIDTH), lambda i: (i // nblk, 0, 0))],
        out_specs=[pl.BlockSpec((t, MOBA_WIDTH), lambda i: (i, 0)),
                   pl.BlockSpec((t, LANES), lambda i: (i, 0))],
        out_shape=[jax.ShapeDtypeStruct((m, MOBA_WIDTH), BF16),
                   jax.ShapeDtypeStruct((m, LANES), BF16)],
        compiler_params=_cparams(("parallel",)),
        name="moba_q_prep",
    )(proj, cos_t, sin_t, km_mat)


def _pair_cols(x, h0, h1, shape):
    lane = lax.broadcasted_iota(jnp.int32, shape, 1)
    a = jnp.broadcast_to(x[:, h0:h0 + 1], shape)
    b = jnp.broadcast_to(x[:, h1:h1 + 1], shape)
    return jnp.where(lane < SSM_HEAD_DIM, a, b)


def _ssd_kernel(in_ref, cw_ref, cb_ref, dtb_ref, alog_ref, dsk_ref, nw_ref, o_ref, xpad_scr, st_scr):
    t = SSM_CHUNK
    c = pl.program_id(1)
    halo = SUBLANES

    @pl.when(c == 0)
    def _():
        xpad_scr[0:halo, :] = jnp.zeros((halo, SSM_CONV_DIM), F32)
        st_scr[...] = jnp.zeros(st_scr.shape, F32)

    xpad_scr[halo:halo + t, :] = in_ref[:, 0:SSM_CONV_DIM]
    conv = cb_ref[...] + cw_ref[0:1, :] * xpad_scr[pl.ds(halo - SSM_CONV + 1, t), :]
    for k in range(1, SSM_CONV):
        conv = conv + cw_ref[k:k + 1, :] * xpad_scr[pl.ds(halo - SSM_CONV + 1 + k, t), :]
    xpad_scr[0:halo, :] = xpad_scr[t:t + halo, :]
    xbc = conv * _sigmoid(conv)
    z = in_ref[:, SSM_CONV_DIM:SSM_CONV_DIM + SSM_WIDTH]
    dt = _softplus(in_ref[:, SSM_CONV_DIM + SSM_WIDTH:] + dtb_ref[...])
    a = -jnp.exp(alog_ref[...])
    acum = _tri_cumsum(dt * a)
    acum_t = acum.T
    tot = acum[t - 1:t, :]
    decay_end = jnp.exp(tot - acum)
    decay_in = jnp.exp(acum)
    exp_tot = jnp.exp(tot)
    row = lax.broadcasted_iota(jnp.int32, (t, t), 0)
    col = lax.broadcasted_iota(jnp.int32, (t, t), 1)
    causal = row >= col
    lane = lax.broadcasted_iota(jnp.int32, (t, LANES), 1)
    lane1 = lax.broadcasted_iota(jnp.int32, (1, LANES), 1)
    pair_shape = (t, LANES)
    rep = SSM_HEADS // SSM_GROUPS
    b_off = SSM_WIDTH
    c_off = SSM_WIDTH + SSM_GROUPS * SSM_STATE
    ys = []
    for g in range(SSM_GROUPS):
        bm = xbc[:, b_off + g * SSM_STATE:b_off + (g + 1) * SSM_STATE]
        cm = xbc[:, c_off + g * SSM_STATE:c_off + (g + 1) * SSM_STATE].astype(BF16)
        cb = _dot_nt(cm, bm.astype(BF16))
        bm_t = bm.T.astype(BF16)
        for pr in range(rep // 2):
            j = g * (rep // 2) + pr
            h0, h1 = 2 * j, 2 * j + 1
            xp = xbc[:, j * LANES:(j + 1) * LANES]
            xdt = xp * _pair_cols(dt, h0, h1, pair_shape)
            xdt_b = xdt.astype(BF16)
            yd = []
            for h in (h0, h1):
                diff = acum[:, h:h + 1] - acum_t[h:h + 1, :]
                w = cb * jnp.exp(jnp.where(causal, diff, NEG_BIG))
                yd.append(_dot(w.astype(BF16), xdt_b))
            y_diag = jnp.where(lane < SSM_HEAD_DIM, yd[0], yd[1])
            prev = st_scr[j]
            y_off = _dot(cm, prev.astype(BF16)) * _pair_cols(decay_in, h0, h1, pair_shape)
            xw = (xdt * _pair_cols(decay_end, h0, h1, pair_shape)).astype(BF16)
            scale = jnp.where(lane1 < SSM_HEAD_DIM, exp_tot[:, h0:h0 + 1], exp_tot[:, h1:h1 + 1])
            st_scr[j] = prev * scale + _dot(bm_t, xw)
            ys.append(y_diag + y_off + dsk_ref[:, j * LANES:(j + 1) * LANES] * xp)
    y = jnp.concatenate(ys, axis=1)
    y = y * (z * _sigmoid(z))
    gw = SSM_WIDTH // SSM_GROUPS
    outs = []
    for g in range(SSM_GROUPS):
        seg = y[:, g * gw:(g + 1) * gw]
        ms = jnp.mean(seg * seg, axis=-1, keepdims=True)
        outs.append(seg * lax.rsqrt(ms + NORM_EPS) * nw_ref[:, g * gw:(g + 1) * gw])
    o_ref[...] = jnp.concatenate(outs, axis=1).astype(o_ref.dtype)


def _ssd(proj, conv_w, conv_b, dt_bias, a_log, d_skip, norm_w, seq):
    m, wtot = proj.shape
    t = SSM_CHUNK
    nc = seq // t
    pad = LANES - SSM_HEADS
    dtb = jnp.pad(dt_bias, (0, pad)).reshape(1, LANES)
    alog = jnp.pad(a_log, (0, pad)).reshape(1, LANES)
    dsk = jnp.repeat(d_skip, SSM_HEAD_DIM).reshape(1, SSM_WIDTH)
    const = lambda b, c: (0, 0)
    return pl.pallas_call(
        _ssd_kernel,
        grid=(m // seq, nc),
        in_specs=[pl.BlockSpec((t, wtot), lambda b, c: (b * nc + c, 0)),
                  pl.BlockSpec((SSM_CONV, SSM_CONV_DIM), const),
                  pl.BlockSpec((1, SSM_CONV_DIM), const),
                  pl.BlockSpec((1, LANES), const),
                  pl.BlockSpec((1, LANES), const),
                  pl.BlockSpec((1, SSM_WIDTH), const),
                  pl.BlockSpec((1, SSM_WIDTH), const)],
        out_specs=pl.BlockSpec((t, SSM_WIDTH), lambda b, c: (b * nc + c, 0)),
        out_shape=jax.ShapeDtypeStruct((m, SSM_WIDTH), BF16),
        scratch_shapes=[pltpu.VMEM((t + SUBLANES, SSM_CONV_DIM), F32),
                        pltpu.VMEM((SSM_HEADS // 2, SSM_STATE, LANES), F32)],
        compiler_params=_cparams(("parallel", "arbitrary")),
        name="ssd",
    )(proj, conv_w, conv_b.reshape(1, SSM_CONV_DIM), dtb, alog, dsk, norm_w.reshape(1, SSM_WIDTH))


def _merge_kernel(ya_ref, yb_ref, yc_ref, yd_ref, g_ref, x_ref, g1_ref,
                  wa_ref, wb_ref, wc_ref, wd_ref, wo_ref, o_ref):
    d = x_ref.shape[1]
    gate = lambda i: _sigmoid(g_ref[:, i * d:(i + 1) * d].astype(F32))
    merged = gate(0) * _dot(ya_ref[...], wa_ref[...])
    merged = merged + gate(1) * _dot(yb_ref[...], wb_ref[...])
    merged = merged + gate(2) * _dot(yc_ref[...], wc_ref[...])
    merged = merged + gate(3) * _dot(yd_ref[...], wd_ref[...])
    o_ref[...] = x_ref[...] + g1_ref[0] * _dot(merged.astype(BF16), wo_ref[...])


def _merge(ya, yb, yc, yd, gates, x, mod, k_gate, wa, wb, wc, wd, wo, seq, tm):
    m, d = x.shape
    nb = m // seq
    rows = lambda i: (i, 0)
    const = lambda i: (0, 0)
    return pl.pallas_call(
        _merge_kernel,
        grid=(m // tm,),
        in_specs=[pl.BlockSpec((tm, ya.shape[1]), rows),
                  pl.BlockSpec((tm, yb.shape[1]), rows),
                  pl.BlockSpec((tm, yc.shape[1]), rows),
                  pl.BlockSpec((tm, yd.shape[1]), rows),
                  pl.BlockSpec((tm, N_BRANCH * d), rows),
                  pl.BlockSpec((tm, d), rows),
                  pl.BlockSpec((1, 1, d), lambda i: (k_gate * nb + (i * tm) // seq, 0, 0)),
                  pl.BlockSpec(wa.shape, const),
                  pl.BlockSpec(wb.shape, const),
                  pl.BlockSpec(wc.shape, const),
                  pl.BlockSpec(wd.shape, const),
                  pl.BlockSpec(wo.shape, const)],
        out_specs=pl.BlockSpec((tm, d), rows),
        out_shape=jax.ShapeDtypeStruct((m, d), F32),
        compiler_params=_cparams(("parallel",)),
        name="merge",
    )(ya, yb, yc, yd, gates, x, mod, wa, wb, wc, wd, wo)


def _ffn_kernel(x_ref, nw_ref, sc_ref, sh_ref, g_ref, w1_ref, w2_ref, o_ref, h_scr, acc_scr):
    j = pl.program_id(1)

    @pl.when(j == 0)
    def _():
        h_scr[...] = _rms_mod(x_ref[...], nw_ref[...], sc_ref[0], sh_ref[0]).astype(BF16)
        acc_scr[...] = jnp.zeros(acc_scr.shape, F32)

    a = jnp.maximum(_dot(h_scr[...], w1_ref[...]), 0.0)
    acc_scr[...] += _dot((a * a).astype(BF16), w2_ref[...])

    @pl.when(j == pl.num_programs(1) - 1)
    def _():
        o_ref[...] = x_ref[...] + g_ref[0] * acc_scr[...]


def _ffn(x, nw, mod, k_sc, k_sh, k_gate, w1, w2, seq, tm, tf):
    m, d = x.shape
    nb = m // seq
    dff = w1.shape[1]
    modrow = lambda k: (lambda i, j: (k * nb + (i * tm) // seq, 0, 0))
    return pl.pallas_call(
        _ffn_kernel,
        grid=(m // tm, dff // tf),
        in_specs=[pl.BlockSpec((tm, d), lambda i, j: (i, 0)),
                  pl.BlockSpec((1, d), lambda i, j: (0, 0)),
                  pl.BlockSpec((1, 1, d), modrow(k_sc)),
                  pl.BlockSpec((1, 1, d), modrow(k_sh)),
                  pl.BlockSpec((1, 1, d), modrow(k_gate)),
                  pl.BlockSpec((d, tf), lambda i, j: (0, j)),
                  pl.BlockSpec((tf, d), lambda i, j: (j, 0))],
        out_specs=pl.BlockSpec((tm, d), lambda i, j: (i, 0)),
        out_shape=jax.ShapeDtypeStruct((m, d), F32),
        scratch_shapes=[pltpu.VMEM((tm, d), BF16), pltpu.VMEM((tm, d), F32)],
        compiler_params=_cparams(("parallel", "arbitrary")),
        name="ffn",
    )(x, nw.reshape(1, d), mod, mod, mod, w1, w2)


def _pad_rows(w, rows):
    return jnp.pad(w, ((0, rows - w.shape[0]), (0, 0)))


def _split_w_in(w_in, l):
    wt = jnp.transpose(w_in, (2, 0, 1))[:, l, :]
    o = 0
    uv = wt[o:o + 2 * GMLP_WIDTH]; o += 2 * GMLP_WIDTH
    fox_qk = wt[o:o + 2 * FOX_WIDTH]; o += 2 * FOX_WIDTH
    fox_v = wt[o:o + FOX_WIDTH]; o += FOX_WIDTH
    fox_f = wt[o:o + FOX_HEADS]; o += FOX_HEADS
    moba_qk = wt[o:o + 2 * MOBA_WIDTH]; o += 2 * MOBA_WIDTH
    moba_v = wt[o:o + MOBA_WIDTH]; o += MOBA_WIDTH
    z = wt[o:o + SSM_WIDTH]; o += SSM_WIDTH
    xbc = wt[o:o + SSM_CONV_DIM]; o += SSM_CONV_DIM
    dt = wt[o:o + SSM_HEADS]; o += SSM_HEADS
    gates = wt[o:]
    w_v = jnp.concatenate([fox_v, moba_v], axis=0)
    w_mqkf = jnp.concatenate([moba_qk, _pad_rows(fox_f, LANES)], axis=0)
    w_ssd = jnp.concatenate([xbc, z, _pad_rows(dt, LANES)], axis=0)
    return [g.astype(BF16) for g in (uv, fox_qk, w_v, w_mqkf, w_ssd, gates)]


def _tile_rows(seq, want):
    return min(want, seq)


def kernel(x, c, ada_w, ada_b, norm_mix_w, w_in, gmlp_ln_w, gmlp_ln_b, gmlp_ws, gmlp_bs, fox_f_bias, ssm_conv_w, ssm_conv_b, ssm_dt_bias, ssm_a_log, ssm_d, ssm_norm_w, w_branch_a, w_branch_b, w_branch_c, w_branch_d, w_out, norm_mlp_w, mlp_w1, mlp_w2, final_norm_w):
    nb, seq, d = x.shape
    m = nb * seq
    depth = ada_w.shape[0]
    tm = _tile_rows(seq, 512)
    tm_proj = _tile_rows(seq, 1024)
    tq_attn = _tile_rows(seq, ATTN_Q_TILE)
    pairs = FOX_HEADS // 2
    nblk = seq // MOBA_BLOCK
    slots = LANES // MOBA_HEADS
    assert seq % MOBA_BLOCK == 0 and nblk <= slots and seq % SSM_CHUNK == 0

    mod_all = _modulation(c, ada_w, ada_b)[:, :nb]
    mod_all = mod_all.reshape(depth, nb, 6, d).transpose(0, 2, 1, 3).reshape(depth, 6 * nb, 1, d)
    cos_t, sin_t = _rope_tables(seq)
    head_eye = jnp.eye(MOBA_HEADS, dtype=F32)

    xf = x.reshape(m, d)
    for l in range(depth):
        mod = mod_all[l]
        w_uv, w_fqk, w_v, w_mqkf, w_ssd, w_gate = _split_w_in(w_in, l)
        h = _norm_mod(xf, norm_mix_w[l], mod, 1, 0, seq, tm)
        p_uv = _matmul(h, w_uv, F32, tm_proj, w_uv.shape[0], "proj_uv")
        p_fqk = _matmul(h, w_fqk, BF16, tm_proj, w_fqk.shape[0], "proj_fox_qk")
        p_vt = _matmul_t(h, w_v, BF16, tm_proj, "proj_vt")
        p_mqkf = _matmul(h, w_mqkf, F32, tm_proj, w_mqkf.shape[0], "proj_mqkf")
        p_ssd = _matmul(h, w_ssd, F32, tm_proj, w_ssd.shape[0], "proj_ssd")
        p_gate = _matmul(h, w_gate, BF16, tm_proj, 2 * d, "proj_gate")

        y_a = _gmlp(p_uv, gmlp_ln_w[l], gmlp_ln_b[l], gmlp_ws[l], gmlp_bs[l], tm)

        k_aux = _forget_cumsum(p_mqkf, 2 * MOBA_WIDTH // LANES, fox_f_bias[l], seq)
        y_b = _block_attention("fox", p_fqk, 0, k_aux, p_fqk, pairs, p_vt, 0, seq, tq_attn, MOBA_BLOCK)

        k_rot, k_mean = _moba_k_prep(p_mqkf, 1, cos_t, sin_t, seq)
        km = jnp.pad(k_mean.reshape(nb, nblk, MOBA_HEADS, HEAD_DIM), ((0, 0), (0, slots - nblk), (0, 0), (0, 0)))
        km_mat = jnp.einsum('bnhd,kh->bknhd', km, head_eye).reshape(nb, LANES, MOBA_WIDTH)
        q_rot, not_sel = _moba_q_prep(p_mqkf, 0, cos_t, sin_t, km_mat, seq)
        y_c = _block_attention("moba", q_rot, 0, not_sel, k_rot, 0, p_vt, pairs, seq, tq_attn, MOBA_BLOCK)

        y_d = _ssd(p_ssd, ssm_conv_w[l], ssm_conv_b[l], ssm_dt_bias[l], ssm_a_log[l], ssm_d[l],
                   ssm_norm_w[l], seq)

        xf = _merge(y_a, y_b, y_c, y_d, p_gate, xf, mod, 2,
                    w_branch_a[l].astype(BF16), w_branch_b[l].astype(BF16), w_branch_c[l].astype(BF16),
                    w_branch_d[l].astype(BF16), w_out[l].astype(BF16), seq, tm)
        xf = _ffn(xf, norm_mlp_w[l], mod, 4, 3, 5, mlp_w1[l].astype(BF16), mlp_w2[l].astype(BF16),
                  seq, _tile_rows(seq, 1024), 1024)
    return _final_norm(xf, final_norm_w, tm).reshape(nb, seq, d)
```

```python
import functools
import math

import jax
import jax.numpy as jnp
from jax import lax
from jax.experimental import pallas as pl
from jax.experimental.pallas import tpu as pltpu

F32 = jnp.float32
BF16 = jnp.bfloat16

HEAD_DIM = 64
NORM_EPS = 1e-6
GMLP_GROUPS = 8
GMLP_WIDTH = GMLP_GROUPS * HEAD_DIM
GMLP_CHUNK = 128
FOX_HEADS = 8
FOX_WIDTH = FOX_HEADS * HEAD_DIM
MOBA_HEADS = 8
MOBA_WIDTH = MOBA_HEADS * HEAD_DIM
MOBA_BLOCK = 256
MOBA_TOPK = 3
ROPE_THETA = 500000.0
ROPE_DIM = HEAD_DIM // 4
SSM_HEADS = 12
SSM_HEAD_DIM = 64
SSM_WIDTH = SSM_HEADS * SSM_HEAD_DIM
SSM_GROUPS = 2
SSM_STATE = 128
SSM_CONV = 4
SSM_CHUNK = 128
SSM_CONV_DIM = SSM_WIDTH + 2 * SSM_GROUPS * SSM_STATE
N_BRANCH = 4

LANES = 128
SUBLANES = 8
NEG_BIG = -1e30
VMEM_LIMIT = 48 * 1024 * 1024
FFN_VMEM_LIMIT = 56 * 1024 * 1024
ATTN_Q_TILE = 512


def _cparams(sem, vmem_limit=VMEM_LIMIT):
    return pltpu.CompilerParams(dimension_semantics=sem, vmem_limit_bytes=vmem_limit)


def _sigmoid(x):
    return 1.0 / (1.0 + jnp.exp(-x))


def _softplus(x):
    return jnp.maximum(x, 0.0) + jnp.log(1.0 + jnp.exp(-jnp.abs(x)))


def _dot(a, b):
    return jnp.dot(a, b, preferred_element_type=F32)


def _dot_nt(a, b):
    return lax.dot_general(a, b, (((1,), (1,)), ((), ())), preferred_element_type=F32)


def _keep_high_half(x):
    bits = lax.bitcast_convert_type(x, jnp.uint32) & jnp.uint32(0xFFFF0000)
    return lax.bitcast_convert_type(bits, F32)


def _split3(x):
    hi = _keep_high_half(x)
    r = x - hi
    mid = _keep_high_half(r)
    lo = r - mid
    return hi.astype(BF16), mid.astype(BF16), lo.astype(BF16)


def _tri_cumsum(x):
    t = x.shape[0]
    row = lax.broadcasted_iota(jnp.int32, (t, t), 0)
    col = lax.broadcasted_iota(jnp.int32, (t, t), 1)
    tri = jnp.where(row >= col, 1.0, 0.0).astype(BF16)
    hi, mid, lo = _split3(x)
    return _dot(tri, hi) + _dot(tri, mid) + _dot(tri, lo)


def _six_pass(dot, a, b):
    ah, am, al = _split3(a)
    bh, bm, bl = _split3(b)
    small = dot(am, bm) + dot(ah, bl) + dot(al, bh)
    return dot(ah, bh) + (dot(ah, bm) + dot(am, bh)) + small


def _dot_f32(a, b):
    return _six_pass(_dot, a, b)


def _dot_f32_nt(a, b):
    return _six_pass(_dot_nt, a, b)


def _mod_kernel(c_ref, w_ref, b_ref, o_ref):
    c = c_ref[...]
    ca = c * _sigmoid(c)
    o_ref[0] = _dot_f32(ca, w_ref[0]) + b_ref[0]


def _modulation(c, ada_w, ada_b):
    nb, d = c.shape
    nl = ada_w.shape[0]
    bp = -(-nb // SUBLANES) * SUBLANES
    cp = jnp.pad(c, ((0, bp - nb), (0, 0)))
    return pl.pallas_call(
        _mod_kernel,
        grid=(nl, 6),
        in_specs=[pl.BlockSpec((bp, d), lambda l, k: (0, 0)),
                  pl.BlockSpec((1, d, d), lambda l, k: (l, 0, k)),
                  pl.BlockSpec((1, 1, d), lambda l, k: (l, 0, k))],
        out_specs=pl.BlockSpec((1, bp, d), lambda l, k: (l, 0, k)),
        out_shape=jax.ShapeDtypeStruct((nl, bp, 6 * d), F32),
        compiler_params=_cparams(("parallel", "parallel")),
        name="adaln_mod",
    )(cp, ada_w, ada_b.reshape(nl, 1, 6 * d))


def _rms_mod(x, nw, sc, sh):
    ms = jnp.mean(x * x, axis=-1, keepdims=True)
    xn = x * lax.rsqrt(ms + NORM_EPS)
    return xn * nw * (1.0 + sc) + sh


def _norm_kernel(x_ref, nw_ref, sc_ref, sh_ref, o_ref):
    o_ref[...] = _rms_mod(x_ref[...], nw_ref[...], sc_ref[0], sh_ref[0]).astype(o_ref.dtype)


def _norm_mod(x, nw, mod, k_sc, k_sh, seq, tm):
    m, d = x.shape
    nb = m // seq
    return pl.pallas_call(
        _norm_kernel,
        grid=(m // tm,),
        in_specs=[pl.BlockSpec((tm, d), lambda i: (i, 0)),
                  pl.BlockSpec((1, d), lambda i: (0, 0)),
                  pl.BlockSpec((1, 1, d), lambda i: (k_sc * nb + (i * tm) // seq, 0, 0)),
                  pl.BlockSpec((1, 1, d), lambda i: (k_sh * nb + (i * tm) // seq, 0, 0))],
        out_specs=pl.BlockSpec((tm, d), lambda i: (i, 0)),
        out_shape=jax.ShapeDtypeStruct((m, d), BF16),
        compiler_params=_cparams(("parallel",)),
        name="norm_mod",
    )(x, nw.reshape(1, d), mod, mod)


def _final_norm_kernel(x_ref, nw_ref, o_ref):
    x = x_ref[...]
    ms = jnp.mean(x * x, axis=-1, keepdims=True)
    o_ref[...] = x * lax.rsqrt(ms + NORM_EPS) * nw_ref[...]


def _final_norm(x, nw, tm):
    m, d = x.shape
    return pl.pallas_call(
        _final_norm_kernel,
        grid=(m // tm,),
        in_specs=[pl.BlockSpec((tm, d), lambda i: (i, 0)),
                  pl.BlockSpec((1, d), lambda i: (0, 0))],
        out_specs=pl.BlockSpec((tm, d), lambda i: (i, 0)),
        out_shape=jax.ShapeDtypeStruct((m, d), F32),
        compiler_params=_cparams(("parallel",)),
        name="final_norm",
    )(x, nw.reshape(1, d))


def _mm_kernel(a_ref, wt_ref, o_ref):
    o_ref[...] = _dot_nt(a_ref[...], wt_ref[...]).astype(o_ref.dtype)


def _matmul(a, wt, out_dtype, tm, tn, name):
    m, k = a.shape
    n = wt.shape[0]
    return pl.pallas_call(
        _mm_kernel,
        grid=(n // tn, m // tm),
        in_specs=[pl.BlockSpec((tm, k), lambda j, i: (i, 0)),
                  pl.BlockSpec((tn, k), lambda j, i: (j, 0))],
        out_specs=pl.BlockSpec((tm, tn), lambda j, i: (i, j)),
        out_shape=jax.ShapeDtypeStruct((m, n), out_dtype),
        compiler_params=_cparams(("parallel", "parallel")),
        name=name,
    )(a, wt)


def _mm_t_kernel(a_ref, wt_ref, o_ref):
    o_ref[...] = _dot_nt(wt_ref[...], a_ref[...]).astype(o_ref.dtype)


def _matmul_t(a, wt, out_dtype, tm, name):
    m, k = a.shape
    n = wt.shape[0]
    return pl.pallas_call(
        _mm_t_kernel,
        grid=(m // tm,),
        in_specs=[pl.BlockSpec((tm, k), lambda i: (i, 0)),
                  pl.BlockSpec((n, k), lambda i: (0, 0))],
        out_specs=pl.BlockSpec((n, tm), lambda i: (0, i)),
        out_shape=jax.ShapeDtypeStruct((n, m), out_dtype),
        compiler_params=_cparams(("parallel",)),
        name=name,
    )(a, wt)


def _gelu_tanh(x):
    c = math.sqrt(2.0 / math.pi)
    return 0.5 * x * (1.0 + jnp.tanh(c * (x + 0.044715 * (x * x * x))))


def _gmlp_kernel(uv_ref, lnw_ref, lnb_ref, ws_ref, bs_ref, o_ref, *, tm):
    g = _gelu_tanh(uv_ref[...])
    u = g[:, :GMLP_WIDTH]
    v = g[:, GMLP_WIDTH:]
    mu = jnp.mean(v, axis=-1, keepdims=True)
    vc = v - mu
    var = jnp.mean(vc * vc, axis=-1, keepdims=True)
    vn = (vc * lax.rsqrt(var + NORM_EPS) * lnw_ref[...] + lnb_ref[...]).astype(BF16)
    t = GMLP_CHUNK
    row = lax.broadcasted_iota(jnp.int32, (t, t), 0)
    col = lax.broadcasted_iota(jnp.int32, (t, t), 1)
    causal = row >= col
    ws = [jnp.where(causal, ws_ref[i], 0.0).astype(BF16) for i in range(GMLP_GROUPS)]
    first_head = lax.broadcasted_iota(jnp.int32, (t, LANES), 1) < HEAD_DIM
    for c in range(tm // t):
        rows = slice(c * t, (c + 1) * t)
        for p in range(GMLP_WIDTH // LANES):
            cols = slice(p * LANES, (p + 1) * LANES)
            vp = vn[rows, cols]
            mixed = jnp.where(first_head, _dot(ws[2 * p], vp), _dot(ws[2 * p + 1], vp))
            o_ref[rows, cols] = (u[rows, cols] * (mixed + bs_ref[:, cols])).astype(o_ref.dtype)


def _gmlp(uv, ln_w, ln_b, ws, bs, tm):
    m = uv.shape[0]
    w = GMLP_WIDTH
    bs_full = jnp.repeat(bs.T, HEAD_DIM, axis=1)
    return pl.pallas_call(
        functools.partial(_gmlp_kernel, tm=tm),
        grid=(m // tm,),
        in_specs=[pl.BlockSpec((tm, 2 * w), lambda i: (i, 0)),
                  pl.BlockSpec((1, w), lambda i: (0, 0)),
                  pl.BlockSpec((1, w), lambda i: (0, 0)),
                  pl.BlockSpec((GMLP_GROUPS, GMLP_CHUNK, GMLP_CHUNK), lambda i: (0, 0, 0)),
                  pl.BlockSpec((GMLP_CHUNK, w), lambda i: (0, 0))],
        out_specs=pl.BlockSpec((tm, w), lambda i: (i, 0)),
        out_shape=jax.ShapeDtypeStruct((m, w), BF16),
        compiler_params=_cparams(("parallel",)),
        name="gmlp",
    )(uv, ln_w.reshape(1, w), ln_b.reshape(1, w), ws, bs_full)


def _fcum_kernel(f_ref, b_ref, o_ref, *, seq):
    t = LANES
    r = lax.broadcasted_iota(jnp.int32, (LANES, LANES), 0)
    c = lax.broadcasted_iota(jnp.int32, (LANES, LANES), 1)
    spread = [jnp.where((c == 3 * r + part) & (r < FOX_HEADS), 1.0, 0.0).astype(BF16) for part in range(3)]

    def body(i, carry):
        r0 = pl.multiple_of(i * t, t)
        z = f_ref[pl.ds(r0, t), :] + b_ref[...]
        logf = jnp.minimum(z, 0.0) - jnp.log(1.0 + jnp.exp(-jnp.abs(z)))
        cs = _tri_cumsum(logf) + carry
        hi, mid, lo = _split3(-cs)
        o_ref[pl.ds(r0, t), :] = (_dot(hi, spread[0]) + _dot(mid, spread[1]) + _dot(lo, spread[2])).astype(o_ref.dtype)
        return cs[t - 1:t, :]

    lax.fori_loop(0, seq // t, body, jnp.zeros((1, LANES), F32))


def _forget_cumsum(proj, col_block, f_bias, seq):
    m = proj.shape[0]
    fb = jnp.pad(f_bias, (0, LANES - FOX_HEADS)).reshape(1, LANES)
    return pl.pallas_call(
        functools.partial(_fcum_kernel, seq=seq),
        grid=(m // seq,),
        in_specs=[pl.BlockSpec((seq, LANES), lambda b: (b, col_block)),
                  pl.BlockSpec((1, LANES), lambda b: (0, 0))],
        out_specs=pl.BlockSpec((seq, LANES), lambda b: (b, 0)),
        out_shape=jax.ShapeDtypeStruct((m, LANES), BF16),
        compiler_params=_cparams(("parallel",)),
        name="fox_cumsum",
    )(proj, fb)


def _attn_kernel(*refs, tq, tk, nblk, mode):
    if mode == "fox":
        q_ref, k_ref, ka_ref, vt_ref, o_ref, vp_scr, acc_scr, st_scr, p_scr, qc_scr = refs
    else:
        q_ref, ns_ref, k_ref, vt_ref, o_ref, vp_scr, acc_scr, st_scr, p_scr, qc_scr = refs
    p = pl.program_id(1)
    qi = pl.program_id(2)
    half = LANES // 2
    n_sub = tq // tk

    @pl.when(qi == 0)
    def _():
        chan = lax.broadcasted_iota(jnp.int32, (LANES, tk), 0)
        for jb in range(nblk):
            vt = vt_ref[:, jb * tk:(jb + 1) * tk]
            ones = jnp.ones_like(vt)
            vp_scr[0, jb] = jnp.where(chan < half, vt, ones)
            vp_scr[1, jb] = jnp.where(chan >= half, vt, ones)

    lane_q = lax.broadcasted_iota(jnp.int32, (tq, LANES), 1)
    lane_k = lax.broadcasted_iota(jnp.int32, (tk, LANES), 1)
    q = q_ref[...]
    if mode == "fox":
        q = q * (HEAD_DIM ** -0.5)
    slots = LANES // MOBA_HEADS
    for h in range(2):
        head_lanes = (lane_q < half) if h == 0 else (lane_q >= half)
        qh = jnp.where(head_lanes, q, jnp.zeros_like(q))
        if mode == "fox":
            first = 3 * (2 * p + h)
            aux = jnp.where((lane_q >= first) & (lane_q < first + 3), 1.0, 0.0).astype(BF16)
        else:
            aux = ns_ref[...]
        qc_scr[h] = jnp.concatenate([qh, aux], axis=1)

    def scores(j):
        r0 = pl.multiple_of(j * tk, tk)
        kb = k_ref[pl.ds(r0, tk), :]
        out = []
        for h in range(2):
            if mode == "fox":
                ka = ka_ref[pl.ds(r0, tk), :]
            else:
                ka = jnp.where(lane_k == (2 * p + h) * slots + j, NEG_BIG, 0.0).astype(BF16)
            out.append(_dot_nt(jnp.concatenate([kb, ka], axis=1), qc_scr[h]))
        return out

    def advance(j, ms, mask, prefetch):
        nxt = scores(j + 1) if prefetch else None
        prev = jnp.maximum(j - 1, 0)
        pv = [_dot(vp_scr[h, prev], p_scr[h]) for h in range(2)]
        ms2 = []
        for h in range(2):
            st = st_scr[h]
            if mask is not None:
                st = jnp.where(mask, st, NEG_BIG)
            m_new = jnp.maximum(ms[h], jnp.max(st, axis=0, keepdims=True))
            alpha = jnp.exp(ms[h] - m_new)
            pt = jnp.exp(st - m_new).astype(BF16)
            acc_scr[h] = alpha * (acc_scr[h] + pv[h])
            p_scr[h] = pt
            ms2.append(m_new)
        if prefetch:
            for h in range(2):
                st_scr[h] = nxt[h]
        return tuple(ms2)

    acc_scr[...] = jnp.zeros(acc_scr.shape, F32)
    p_scr[...] = jnp.zeros(p_scr.shape, BF16)
    first_scores = scores(0)
    for h in range(2):
        st_scr[h] = first_scores[h]
    m0 = jnp.full((1, tq), -jnp.inf, F32)
    first_diag = qi * n_sub

    def body(i, ms):
        for s in range(n_sub):
            ms = advance(i * n_sub + s, ms, None, True)
        return ms

    ms = lax.fori_loop(0, qi, body, (m0, m0))
    key_i = lax.broadcasted_iota(jnp.int32, (tk, tq), 0)
    qry_i = lax.broadcasted_iota(jnp.int32, (tk, tq), 1)
    for s in range(n_sub):
        ms = advance(first_diag + s, ms, (key_i + s * tk) <= qry_i, s + 1 < n_sub)
    last = first_diag + n_sub - 1
    outs = []
    for h in range(2):
        a = acc_scr[h] + _dot(vp_scr[h, last], p_scr[h])
        denom = a[(1 - h) * half:(1 - h) * half + 1, :]
        outs.append(a[h * half:(h + 1) * half, :] / denom)
    o_ref[...] = jnp.concatenate(outs, axis=0).T.astype(o_ref.dtype)


def _block_attention(mode, q_arr, q_col, aux_arr, k_arr, k_col, vt_arr, vt_row, seq, tq, tk):
    m = q_arr.shape[0]
    nb = m // seq
    nq = seq // tq
    pairs = FOX_HEADS // 2
    q_spec = pl.BlockSpec((tq, LANES), lambda b, p, i: (b * nq + i, q_col + p))
    k_spec = pl.BlockSpec((seq, LANES), lambda b, p, i: (b, k_col + p))
    vt_spec = pl.BlockSpec((LANES, seq), lambda b, p, i: (vt_row + p, b))
    if mode == "fox":
        in_specs = [q_spec, k_spec, pl.BlockSpec((seq, LANES), lambda b, p, i: (b, 0)), vt_spec]
        args = (q_arr, k_arr, aux_arr, vt_arr)
    else:
        in_specs = [q_spec, pl.BlockSpec((tq, LANES), lambda b, p, i: (b * nq + i, 0)), k_spec, vt_spec]
        args = (q_arr, aux_arr, k_arr, vt_arr)
    return pl.pallas_call(
        functools.partial(_attn_kernel, tq=tq, tk=tk, nblk=seq // tk, mode=mode),
        grid=(nb, pairs, nq),
        in_specs=in_specs,
        out_specs=pl.BlockSpec((tq, LANES), lambda b, p, i: (b * nq + i, p)),
        out_shape=jax.ShapeDtypeStruct((m, pairs * LANES), BF16),
        scratch_shapes=[pltpu.VMEM((2, seq // tk, LANES, tk), BF16), pltpu.VMEM((2, LANES, tq), F32),
                        pltpu.VMEM((2, tk, tq), F32), pltpu.VMEM((2, tk, tq), BF16),
                        pltpu.VMEM((2, tq, 2 * LANES), BF16)],
        compiler_params=_cparams(("arbitrary", "arbitrary", "arbitrary")),
        name=mode + "_attn",
    )(*args)


def _rope_tables(seq):
    half = ROPE_DIM // 2
    inv_freq = ROPE_THETA ** (-jnp.arange(half, dtype=F32) / half)
    ang = jnp.arange(seq, dtype=F32)[:, None] * inv_freq[None, :]
    cos, sin = jnp.cos(ang), jnp.sin(ang)
    ones = jnp.ones((seq, HEAD_DIM - ROPE_DIM), F32)
    cos_h = jnp.concatenate([cos, cos, ones], axis=1)
    sin_h = jnp.concatenate([-sin, sin, 0.0 * ones], axis=1)
    return jnp.tile(cos_h, (1, 2)), jnp.tile(sin_h, (1, 2))


def _rotary(x, cos_t, sin_t):
    half = ROPE_DIM // 2
    lane = lax.broadcasted_iota(jnp.int32, (x.shape[0], LANES), 1)
    first_half = (lane & (HEAD_DIM - 1)) < half
    outs = []
    for j in range(x.shape[1] // LANES):
        xc = x[:, j * LANES:(j + 1) * LANES]
        up = pltpu.roll(xc, LANES - half, axis=1)
        down = pltpu.roll(xc, half, axis=1)
        outs.append(xc * cos_t + jnp.where(first_half, up, down) * sin_t)
    return jnp.concatenate(outs, axis=1)


def _moba_k_kernel(k_ref, cos_ref, sin_ref, ko_ref, km_ref):
    kr = _rotary(k_ref[...], cos_ref[...], sin_ref[...])
    ko_ref[...] = kr.astype(ko_ref.dtype)
    km_ref[0] = jnp.mean(kr, axis=0, keepdims=True)


def _moba_k_prep(proj, k_col, cos_t, sin_t, seq):
    m = proj.shape[0]
    t = MOBA_BLOCK
    nblk = seq // t
    return pl.pallas_call(
        _moba_k_kernel,
        grid=(m // t,),
        in_specs=[pl.BlockSpec((t, MOBA_WIDTH), lambda i: (i, k_col)),
                  pl.BlockSpec((t, LANES), lambda i: (i % nblk, 0)),
                  pl.BlockSpec((t, LANES), lambda i: (i % nblk, 0))],
        out_specs=[pl.BlockSpec((t, MOBA_WIDTH), lambda i: (i, 0)),
                   pl.BlockSpec((1, 1, MOBA_WIDTH), lambda i: (i, 0, 0))],
        out_shape=[jax.ShapeDtypeStruct((m, MOBA_WIDTH), BF16),
                   jax.ShapeDtypeStruct((m // t, 1, MOBA_WIDTH), F32)],
        compiler_params=_cparams(("parallel",)),
        name="moba_k_prep",
    )(proj, cos_t, sin_t)


def _moba_q_kernel(q_ref, cos_ref, sin_ref, km_ref, qo_ref, ns_ref, *, nblk):
    own = pl.program_id(0) % nblk
    qr = _rotary(q_ref[...], cos_ref[...], sin_ref[...])
    qo_ref[...] = (qr * (HEAD_DIM ** -0.5)).astype(qo_ref.dtype)
    gate_t = _dot_f32_nt(km_ref[0], qr)
    slots = LANES // MOBA_HEADS
    n = lax.broadcasted_iota(jnp.int32, (slots, gate_t.shape[1]), 0)
    flags = []
    for h in range(MOBA_HEADS):
        g = gate_t[h * slots:(h + 1) * slots, :]
        cnt = jnp.zeros(g.shape, jnp.int32)
        for cand in range(slots):
            other = g[cand:cand + 1, :]
            beats = (other > g) | ((other == g) & (n > cand))
            cnt = cnt + jnp.where(beats, jnp.where(cand < own, 1, 0), 0)
        selected = (n == own) | ((n < own) & (cnt < MOBA_TOPK))
        flags.append(jnp.where(selected, 0.0, 1.0))
    ns_ref[...] = jnp.concatenate(flags, axis=0).T.astype(ns_ref.dtype)


def _moba_q_prep(proj, q_col, cos_t, sin_t, km_mat, seq):
    m = proj.shape[0]
    t = MOBA_BLOCK
    nblk = seq // t
    return pl.pallas_call(
        functools.partial(_moba_q_kernel, nblk=nblk),
        grid=(m // t,),
        in_specs=[pl.BlockSpec((t, MOBA_WIDTH), lambda i: (i, q_col)),
                  pl.BlockSpec((t, LANES), lambda i: (i % nblk, 0)),
                  pl.BlockSpec((t, LANES), lambda i: (i % nblk, 0)),
                  pl.BlockSpec((1, LANES, MOBA_WIDTH), lambda i: (i // nblk, 0, 0))],
        out_specs=[pl.BlockSpec((t, MOBA_WIDTH), lambda i: (i, 0)),
                   pl.BlockSpec((t, LANES), lambda i: (i, 0))],
        out_shape=[jax.ShapeDtypeStruct((m, MOBA_WIDTH), BF16),
                   jax.ShapeDtypeStruct((m, LANES), BF16)],
        compiler_params=_cparams(("parallel",)),
        name="moba_q_prep",
    )(proj, cos_t, sin_t, km_mat)


def _pair_cols(x, h0, h1, shape):
    lane = lax.broadcasted_iota(jnp.int32, shape, 1)
    a = jnp.broadcast_to(x[:, h0:h0 + 1], shape)
    b = jnp.broadcast_to(x[:, h1:h1 + 1], shape)
    return jnp.where(lane < SSM_HEAD_DIM, a, b)


def _ssd_kernel(in_ref, cw_ref, cb_ref, dtb_ref, alog_ref, dsk_ref, nw_ref, o_ref, xpad_scr, st_scr):
    t = SSM_CHUNK
    c = pl.program_id(1)
    halo = SUBLANES

    @pl.when(c == 0)
    def _():
        xpad_scr[0:halo, :] = jnp.zeros((halo, SSM_CONV_DIM), F32)
        st_scr[...] = jnp.zeros(st_scr.shape, F32)

    xpad_scr[halo:halo + t, :] = in_ref[:, 0:SSM_CONV_DIM]
    conv = cb_ref[...] + cw_ref[0:1, :] * xpad_scr[pl.ds(halo - SSM_CONV + 1, t), :]
    for k in range(1, SSM_CONV):
        conv = conv + cw_ref[k:k + 1, :] * xpad_scr[pl.ds(halo - SSM_CONV + 1 + k, t), :]
    xpad_scr[0:halo, :] = xpad_scr[t:t + halo, :]
    xbc = conv * _sigmoid(conv)
    z = in_ref[:, SSM_CONV_DIM:SSM_CONV_DIM + SSM_WIDTH]
    dt = _softplus(in_ref[:, SSM_CONV_DIM + SSM_WIDTH:] + dtb_ref[...])
    a = -jnp.exp(alog_ref[...])
    acum = _tri_cumsum(dt * a)
    acum_t = acum.T
    tot = acum[t - 1:t, :]
    decay_end = jnp.exp(tot - acum)
    decay_in = jnp.exp(acum)
    exp_tot = jnp.exp(tot)
    row = lax.broadcasted_iota(jnp.int32, (t, t), 0)
    col = lax.broadcasted_iota(jnp.int32, (t, t), 1)
    causal = row >= col
    lane = lax.broadcasted_iota(jnp.int32, (t, LANES), 1)
    lane1 = lax.broadcasted_iota(jnp.int32, (1, LANES), 1)
    pair_shape = (t, LANES)
    rep = SSM_HEADS // SSM_GROUPS
    b_off = SSM_WIDTH
    c_off = SSM_WIDTH + SSM_GROUPS * SSM_STATE
    ys = []
    for g in range(SSM_GROUPS):
        bm = xbc[:, b_off + g * SSM_STATE:b_off + (g + 1) * SSM_STATE]
        cm = xbc[:, c_off + g * SSM_STATE:c_off + (g + 1) * SSM_STATE].astype(BF16)
        cb = _dot_nt(cm, bm.astype(BF16))
        bm_t = bm.T.astype(BF16)
        for pr in range(rep // 2):
            j = g * (rep // 2) + pr
            h0, h1 = 2 * j, 2 * j + 1
            xp = xbc[:, j * LANES:(j + 1) * LANES]
            xdt = xp * _pair_cols(dt, h0, h1, pair_shape)
            xdt_b = xdt.astype(BF16)
            yd = []
            for h in (h0, h1):
                diff = acum[:, h:h + 1] - acum_t[h:h + 1, :]
                w = cb * jnp.exp(jnp.where(causal, diff, NEG_BIG))
                yd.append(_dot(w.astype(BF16), xdt_b))
            y_diag = jnp.where(lane < SSM_HEAD_DIM, yd[0], yd[1])
            prev = st_scr[j]
            y_off = _dot(cm, prev.astype(BF16)) * _pair_cols(decay_in, h0, h1, pair_shape)
            xw = (xdt * _pair_cols(decay_end, h0, h1, pair_shape)).astype(BF16)
            scale = jnp.where(lane1 < SSM_HEAD_DIM, exp_tot[:, h0:h0 + 1], exp_tot[:, h1:h1 + 1])
            st_scr[j] = prev * scale + _dot(bm_t, xw)
            ys.append(y_diag + y_off + dsk_ref[:, j * LANES:(j + 1) * LANES] * xp)
    y = jnp.concatenate(ys, axis=1)
    y = y * (z * _sigmoid(z))
    gw = SSM_WIDTH // SSM_GROUPS
    outs = []
    for g in range(SSM_GROUPS):
        seg = y[:, g * gw:(g + 1) * gw]
        ms = jnp.mean(seg * seg, axis=-1, keepdims=True)
        outs.append(seg * lax.rsqrt(ms + NORM_EPS) * nw_ref[:, g * gw:(g + 1) * gw])
    o_ref[...] = jnp.concatenate(outs, axis=1).astype(o_ref.dtype)


def _ssd(proj, conv_w, conv_b, dt_bias, a_log, d_skip, norm_w, seq):
    m, wtot = proj.shape
    t = SSM_CHUNK
    nc = seq // t
    pad = LANES - SSM_HEADS
    dtb = jnp.pad(dt_bias, (0, pad)).reshape(1, LANES)
    alog = jnp.pad(a_log, (0, pad)).reshape(1, LANES)
    dsk = jnp.repeat(d_skip, SSM_HEAD_DIM).reshape(1, SSM_WIDTH)
    const = lambda b, c: (0, 0)
    return pl.pallas_call(
        _ssd_kernel,
        grid=(m // seq, nc),
        in_specs=[pl.BlockSpec((t, wtot), lambda b, c: (b * nc + c, 0)),
                  pl.BlockSpec((SSM_CONV, SSM_CONV_DIM), const),
                  pl.BlockSpec((1, SSM_CONV_DIM), const),
                  pl.BlockSpec((1, LANES), const),
                  pl.BlockSpec((1, LANES), const),
                  pl.BlockSpec((1, SSM_WIDTH), const),
                  pl.BlockSpec((1, SSM_WIDTH), const)],
        out_specs=pl.BlockSpec((t, SSM_WIDTH), lambda b, c: (b * nc + c, 0)),
        out_shape=jax.ShapeDtypeStruct((m, SSM_WIDTH), BF16),
        scratch_shapes=[pltpu.VMEM((t + SUBLANES, SSM_CONV_DIM), F32),
                        pltpu.VMEM((SSM_HEADS // 2, SSM_STATE, LANES), F32)],
        compiler_params=_cparams(("parallel", "arbitrary")),
        name="ssd",
    )(proj, conv_w, conv_b.reshape(1, SSM_CONV_DIM), dtb, alog, dsk, norm_w.reshape(1, SSM_WIDTH))


def _merge_kernel(ya_ref, yb_ref, yc_ref, yd_ref, g_ref, x_ref, g1_ref,
                  wa_ref, wb_ref, wc_ref, wd_ref, wo_ref, o_ref):
    d = x_ref.shape[1]
    gate = lambda i: _sigmoid(g_ref[:, i * d:(i + 1) * d].astype(F32))
    merged = gate(0) * _dot(ya_ref[...], wa_ref[...])
    merged = merged + gate(1) * _dot(yb_ref[...], wb_ref[...])
    merged = merged + gate(2) * _dot(yc_ref[...], wc_ref[...])
    merged = merged + gate(3) * _dot(yd_ref[...], wd_ref[...])
    o_ref[...] = x_ref[...] + g1_ref[0] * _dot(merged.astype(BF16), wo_ref[...])


def _merge(ya, yb, yc, yd, gates, x, mod, k_gate, wa, wb, wc, wd, wo, seq, tm):
    m, d = x.shape
    nb = m // seq
    rows = lambda i: (i, 0)
    const = lambda i: (0, 0)
    return pl.pallas_call(
        _merge_kernel,
        grid=(m // tm,),
        in_specs=[pl.BlockSpec((tm, ya.shape[1]), rows),
                  pl.BlockSpec((tm, yb.shape[1]), rows),
                  pl.BlockSpec((tm, yc.shape[1]), rows),
                  pl.BlockSpec((tm, yd.shape[1]), rows),
                  pl.BlockSpec((tm, N_BRANCH * d), rows),
                  pl.BlockSpec((tm, d), rows),
                  pl.BlockSpec((1, 1, d), lambda i: (k_gate * nb + (i * tm) // seq, 0, 0)),
                  pl.BlockSpec(wa.shape, const),
                  pl.BlockSpec(wb.shape, const),
                  pl.BlockSpec(wc.shape, const),
                  pl.BlockSpec(wd.shape, const),
                  pl.BlockSpec(wo.shape, const)],
        out_specs=pl.BlockSpec((tm, d), rows),
        out_shape=jax.ShapeDtypeStruct((m, d), F32),
        compiler_params=_cparams(("parallel",)),
        name="merge",
    )(ya, yb, yc, yd, gates, x, mod, wa, wb, wc, wd, wo)


def _ffn_kernel(x_ref, nw_ref, sc_ref, sh_ref, g_ref, w1_ref, w2_ref, o_ref, h_scr, acc_scr):
    j = pl.program_id(1)

    @pl.when(j == 0)
    def _():
        h_scr[...] = _rms_mod(x_ref[...], nw_ref[...], sc_ref[0], sh_ref[0]).astype(BF16)
        acc_scr[...] = jnp.zeros(acc_scr.shape, F32)

    a = jnp.maximum(_dot(h_scr[...], w1_ref[...]), 0.0)
    acc_scr[...] += _dot((a * a).astype(BF16), w2_ref[...])

    @pl.when(j == pl.num_programs(1) - 1)
    def _():
        o_ref[...] = x_ref[...] + g_ref[0] * acc_scr[...]


def _ffn(x, nw, mod, k_sc, k_sh, k_gate, w1, w2, seq, tm, tf):
    m, d = x.shape
    nb = m // seq
    dff = w1.shape[1]
    modrow = lambda k: (lambda i, j: (k * nb + (i * tm) // seq, 0, 0))
    return pl.pallas_call(
        _ffn_kernel,
        grid=(m // tm, dff // tf),
        in_specs=[pl.BlockSpec((tm, d), lambda i, j: (i, 0)),
                  pl.BlockSpec((1, d), lambda i, j: (0, 0)),
                  pl.BlockSpec((1, 1, d), modrow(k_sc)),
                  pl.BlockSpec((1, 1, d), modrow(k_sh)),
                  pl.BlockSpec((1, 1, d), modrow(k_gate)),
                  pl.BlockSpec((d, tf), lambda i, j: (0, j)),
                  pl.BlockSpec((tf, d), lambda i, j: (j, 0))],
        out_specs=pl.BlockSpec((tm, d), lambda i, j: (i, 0)),
        out_shape=jax.ShapeDtypeStruct((m, d), F32),
        scratch_shapes=[pltpu.VMEM((tm, d), BF16), pltpu.VMEM((tm, d), F32)],
        compiler_params=_cparams(("parallel", "arbitrary"), FFN_VMEM_LIMIT),
        name="ffn",
    )(x, nw.reshape(1, d), mod, mod, mod, w1, w2)


def _pad_rows(w, rows):
    return jnp.pad(w, ((0, rows - w.shape[0]), (0, 0)))


def _split_w_in(w_in, l):
    wt = jnp.transpose(w_in, (2, 0, 1))[:, l, :]
    o = 0
    uv = wt[o:o + 2 * GMLP_WIDTH]; o += 2 * GMLP_WIDTH
    fox_qk = wt[o:o + 2 * FOX_WIDTH]; o += 2 * FOX_WIDTH
    fox_v = wt[o:o + FOX_WIDTH]; o += FOX_WIDTH
    fox_f = wt[o:o + FOX_HEADS]; o += FOX_HEADS
    moba_qk = wt[o:o + 2 * MOBA_WIDTH]; o += 2 * MOBA_WIDTH
    moba_v = wt[o:o + MOBA_WIDTH]; o += MOBA_WIDTH
    z = wt[o:o + SSM_WIDTH]; o += SSM_WIDTH
    xbc = wt[o:o + SSM_CONV_DIM]; o += SSM_CONV_DIM
    dt = wt[o:o + SSM_HEADS]; o += SSM_HEADS
    gates = wt[o:]
    w_v = jnp.concatenate([fox_v, moba_v], axis=0)
    w_mqkf = jnp.concatenate([moba_qk, _pad_rows(fox_f, LANES)], axis=0)
    w_ssd = jnp.concatenate([xbc, z, _pad_rows(dt, LANES)], axis=0)
    return [g.astype(BF16) for g in (uv, fox_qk, w_v, w_mqkf, w_ssd, gates)]


def _tile_rows(seq, want):
    return min(want, seq)


def kernel(x, c, ada_w, ada_b, norm_mix_w, w_in, gmlp_ln_w, gmlp_ln_b, gmlp_ws, gmlp_bs, fox_f_bias, ssm_conv_w, ssm_conv_b, ssm_dt_bias, ssm_a_log, ssm_d, ssm_norm_w, w_branch_a, w_branch_b, w_branch_c, w_branch_d, w_out, norm_mlp_w, mlp_w1, mlp_w2, final_norm_w):
    nb, seq, d = x.shape
    m = nb * seq
    depth = ada_w.shape[0]
    tm = _tile_rows(seq, 512)
    tm_proj = _tile_rows(seq, 1024)
    tq_attn = _tile_rows(seq, ATTN_Q_TILE)
    pairs = FOX_HEADS // 2
    nblk = seq // MOBA_BLOCK
    slots = LANES // MOBA_HEADS
    assert seq % MOBA_BLOCK == 0 and nblk <= slots and seq % SSM_CHUNK == 0

    mod_all = _modulation(c, ada_w, ada_b)[:, :nb]
    mod_all = mod_all.reshape(depth, nb, 6, d).transpose(0, 2, 1, 3).reshape(depth, 6 * nb, 1, d)
    cos_t, sin_t = _rope_tables(seq)
    head_eye = jnp.eye(MOBA_HEADS, dtype=F32)

    xf = x.reshape(m, d)
    for l in range(depth):
        mod = mod_all[l]
        w_uv, w_fqk, w_v, w_mqkf, w_ssd, w_gate = _split_w_in(w_in, l)
        h = _norm_mod(xf, norm_mix_w[l], mod, 1, 0, seq, tm)
        p_uv = _matmul(h, w_uv, F32, tm_proj, w_uv.shape[0], "proj_uv")
        p_fqk = _matmul(h, w_fqk, BF16, tm_proj, w_fqk.shape[0], "proj_fox_qk")
        p_vt = _matmul_t(h, w_v, BF16, tm_proj, "proj_vt")
        p_mqkf = _matmul(h, w_mqkf, F32, tm_proj, w_mqkf.shape[0], "proj_mqkf")
        p_ssd = _matmul(h, w_ssd, F32, tm_proj, w_ssd.shape[0], "proj_ssd")
        p_gate = _matmul(h, w_gate, BF16, tm_proj, w_gate.shape[0], "proj_gate")

        y_a = _gmlp(p_uv, gmlp_ln_w[l], gmlp_ln_b[l], gmlp_ws[l], gmlp_bs[l], tm)

        k_aux = _forget_cumsum(p_mqkf, 2 * MOBA_WIDTH // LANES, fox_f_bias[l], seq)
        y_b = _block_attention("fox", p_fqk, 0, k_aux, p_fqk, pairs, p_vt, 0, seq, tq_attn, MOBA_BLOCK)

        k_rot, k_mean = _moba_k_prep(p_mqkf, 1, cos_t, sin_t, seq)
        km = jnp.pad(k_mean.reshape(nb, nblk, MOBA_HEADS, HEAD_DIM), ((0, 0), (0, slots - nblk), (0, 0), (0, 0)))
        km_mat = jnp.einsum('bnhd,kh->bknhd', km, head_eye).reshape(nb, LANES, MOBA_WIDTH)
        q_rot, not_sel = _moba_q_prep(p_mqkf, 0, cos_t, sin_t, km_mat, seq)
        y_c = _block_attention("moba", q_rot, 0, not_sel, k_rot, 0, p_vt, pairs, seq, tq_attn, MOBA_BLOCK)

        y_d = _ssd(p_ssd, ssm_conv_w[l], ssm_conv_b[l], ssm_dt_bias[l], ssm_a_log[l], ssm_d[l],
                   ssm_norm_w[l], seq)

        xf = _merge(y_a, y_b, y_c, y_d, p_gate, xf, mod, 2,
                    w_branch_a[l].astype(BF16), w_branch_b[l].astype(BF16), w_branch_c[l].astype(BF16),
                    w_branch_d[l].astype(BF16), w_out[l].astype(BF16), seq, tm)
        xf = _ffn(xf, norm_mlp_w[l], mod, 4, 3, 5, mlp_w1[l].astype(BF16), mlp_w2[l].astype(BF16),
                  seq, tm, mlp_w1.shape[2])
    return _final_norm(xf, final_norm_w, tm).reshape(nb, seq, d)
```

```python
import functools
import math

import jax
import jax.numpy as jnp
from jax import lax
from jax.experimental import pallas as pl
from jax.experimental.pallas import tpu as pltpu

F32 = jnp.float32
BF16 = jnp.bfloat16

HEAD_DIM = 64
NORM_EPS = 1e-6
GMLP_GROUPS = 8
GMLP_WIDTH = GMLP_GROUPS * HEAD_DIM
GMLP_CHUNK = 128
FOX_HEADS = 8
FOX_WIDTH = FOX_HEADS * HEAD_DIM
MOBA_HEADS = 8
MOBA_WIDTH = MOBA_HEADS * HEAD_DIM
MOBA_BLOCK = 256
MOBA_TOPK = 3
ROPE_THETA = 500000.0
ROPE_DIM = HEAD_DIM // 4
SSM_HEADS = 12
SSM_HEAD_DIM = 64
SSM_WIDTH = SSM_HEADS * SSM_HEAD_DIM
SSM_GROUPS = 2
SSM_STATE = 128
SSM_CONV = 4
SSM_CHUNK = 128
SSM_CONV_DIM = SSM_WIDTH + 2 * SSM_GROUPS * SSM_STATE
N_BRANCH = 4

LANES = 128
SUBLANES = 8
NEG_BIG = -1e30
VMEM_LIMIT = 48 * 1024 * 1024
FFN_VMEM_LIMIT = 56 * 1024 * 1024
ATTN_Q_TILE = 512


def _cparams(sem, vmem_limit=VMEM_LIMIT):
    return pltpu.CompilerParams(dimension_semantics=sem, vmem_limit_bytes=vmem_limit)


def _sigmoid(x):
    return 1.0 / (1.0 + jnp.exp(-x))


def _softplus(x):
    return jnp.maximum(x, 0.0) + jnp.log(1.0 + jnp.exp(-jnp.abs(x)))


def _dot(a, b):
    return jnp.dot(a, b, preferred_element_type=F32)


def _dot_nt(a, b):
    return lax.dot_general(a, b, (((1,), (1,)), ((), ())), preferred_element_type=F32)


def _keep_high_half(x):
    bits = lax.bitcast_convert_type(x, jnp.uint32) & jnp.uint32(0xFFFF0000)
    return lax.bitcast_convert_type(bits, F32)


def _split3(x):
    hi = _keep_high_half(x)
    r = x - hi
    mid = _keep_high_half(r)
    lo = r - mid
    return hi.astype(BF16), mid.astype(BF16), lo.astype(BF16)


def _tri_cumsum(x):
    t = x.shape[0]
    row = lax.broadcasted_iota(jnp.int32, (t, t), 0)
    col = lax.broadcasted_iota(jnp.int32, (t, t), 1)
    tri = jnp.where(row >= col, 1.0, 0.0).astype(BF16)
    hi, mid, lo = _split3(x)
    return _dot(tri, hi) + _dot(tri, mid) + _dot(tri, lo)


def _six_pass(dot, a, b):
    ah, am, al = _split3(a)
    bh, bm, bl = _split3(b)
    small = dot(am, bm) + dot(ah, bl) + dot(al, bh)
    return dot(ah, bh) + (dot(ah, bm) + dot(am, bh)) + small


def _dot_f32(a, b):
    return _six_pass(_dot, a, b)


def _dot_f32_nt(a, b):
    return _six_pass(_dot_nt, a, b)


def _mod_kernel(c_ref, w_ref, b_ref, o_ref):
    c = c_ref[...]
    ca = c * _sigmoid(c)
    o_ref[0] = _dot_f32(ca, w_ref[0]) + b_ref[0]


def _modulation(c, ada_w, ada_b):
    nb, d = c.shape
    nl = ada_w.shape[0]
    bp = -(-nb // SUBLANES) * SUBLANES
    cp = jnp.pad(c, ((0, bp - nb), (0, 0)))
    return pl.pallas_call(
        _mod_kernel,
        grid=(nl, 6),
        in_specs=[pl.BlockSpec((bp, d), lambda l, k: (0, 0)),
                  pl.BlockSpec((1, d, d), lambda l, k: (l, 0, k)),
                  pl.BlockSpec((1, 1, d), lambda l, k: (l, 0, k))],
        out_specs=pl.BlockSpec((1, bp, d), lambda l, k: (l, 0, k)),
        out_shape=jax.ShapeDtypeStruct((nl, bp, 6 * d), F32),
        compiler_params=_cparams(("parallel", "parallel")),
        name="adaln_mod",
    )(cp, ada_w, ada_b.reshape(nl, 1, 6 * d))


def _rms_mod(x, nw, sc, sh):
    ms = jnp.mean(x * x, axis=-1, keepdims=True)
    xn = x * lax.rsqrt(ms + NORM_EPS)
    return xn * nw * (1.0 + sc) + sh


def _norm_kernel(x_ref, nw_ref, sc_ref, sh_ref, o_ref):
    o_ref[...] = _rms_mod(x_ref[...], nw_ref[...], sc_ref[0], sh_ref[0]).astype(o_ref.dtype)


def _norm_mod(x, nw, mod, k_sc, k_sh, seq, tm):
    m, d = x.shape
    nb = m // seq
    return pl.pallas_call(
        _norm_kernel,
        grid=(m // tm,),
        in_specs=[pl.BlockSpec((tm, d), lambda i: (i, 0)),
                  pl.BlockSpec((1, d), lambda i: (0, 0)),
                  pl.BlockSpec((1, 1, d), lambda i: (k_sc * nb + (i * tm) // seq, 0, 0)),
                  pl.BlockSpec((1, 1, d), lambda i: (k_sh * nb + (i * tm) // seq, 0, 0))],
        out_specs=pl.BlockSpec((tm, d), lambda i: (i, 0)),
        out_shape=jax.ShapeDtypeStruct((m, d), BF16),
        compiler_params=_cparams(("parallel",)),
        name="norm_mod",
    )(x, nw.reshape(1, d), mod, mod)


def _final_norm_kernel(x_ref, nw_ref, o_ref):
    x = x_ref[...]
    ms = jnp.mean(x * x, axis=-1, keepdims=True)
    o_ref[...] = x * lax.rsqrt(ms + NORM_EPS) * nw_ref[...]


def _final_norm(x, nw, tm):
    m, d = x.shape
    return pl.pallas_call(
        _final_norm_kernel,
        grid=(m // tm,),
        in_specs=[pl.BlockSpec((tm, d), lambda i: (i, 0)),
                  pl.BlockSpec((1, d), lambda i: (0, 0))],
        out_specs=pl.BlockSpec((tm, d), lambda i: (i, 0)),
        out_shape=jax.ShapeDtypeStruct((m, d), F32),
        compiler_params=_cparams(("parallel",)),
        name="final_norm",
    )(x, nw.reshape(1, d))


def _mm_kernel(a_ref, wt_ref, o_ref):
    o_ref[...] = _dot_nt(a_ref[...], wt_ref[...]).astype(o_ref.dtype)


def _matmul(a, wt, out_dtype, tm, tn, name):
    m, k = a.shape
    n = wt.shape[0]
    return pl.pallas_call(
        _mm_kernel,
        grid=(n // tn, m // tm),
        in_specs=[pl.BlockSpec((tm, k), lambda j, i: (i, 0)),
                  pl.BlockSpec((tn, k), lambda j, i: (j, 0))],
        out_specs=pl.BlockSpec((tm, tn), lambda j, i: (i, j)),
        out_shape=jax.ShapeDtypeStruct((m, n), out_dtype),
        compiler_params=_cparams(("parallel", "parallel")),
        name=name,
    )(a, wt)


def _mm_t_kernel(a_ref, wt_ref, o_ref):
    o_ref[...] = _dot_nt(wt_ref[...], a_ref[...]).astype(o_ref.dtype)


def _matmul_t(a, wt, out_dtype, tm, name):
    m, k = a.shape
    n = wt.shape[0]
    return pl.pallas_call(
        _mm_t_kernel,
        grid=(m // tm,),
        in_specs=[pl.BlockSpec((tm, k), lambda i: (i, 0)),
                  pl.BlockSpec((n, k), lambda i: (0, 0))],
        out_specs=pl.BlockSpec((n, tm), lambda i: (0, i)),
        out_shape=jax.ShapeDtypeStruct((n, m), out_dtype),
        compiler_params=_cparams(("parallel",)),
        name=name,
    )(a, wt)


def _gelu_tanh(x):
    c = math.sqrt(2.0 / math.pi)
    return 0.5 * x * (1.0 + jnp.tanh(c * (x + 0.044715 * (x * x * x))))


def _gmlp_kernel(uv_ref, lnw_ref, lnb_ref, ws_ref, bs_ref, o_ref, *, tm):
    g = _gelu_tanh(uv_ref[...])
    u = g[:, :GMLP_WIDTH]
    v = g[:, GMLP_WIDTH:]
    mu = jnp.mean(v, axis=-1, keepdims=True)
    vc = v - mu
    var = jnp.mean(vc * vc, axis=-1, keepdims=True)
    vn = (vc * lax.rsqrt(var + NORM_EPS) * lnw_ref[...] + lnb_ref[...]).astype(BF16)
    t = GMLP_CHUNK
    row = lax.broadcasted_iota(jnp.int32, (t, t), 0)
    col = lax.broadcasted_iota(jnp.int32, (t, t), 1)
    causal = row >= col
    ws = [jnp.where(causal, ws_ref[i], 0.0).astype(BF16) for i in range(GMLP_GROUPS)]
    first_head = lax.broadcasted_iota(jnp.int32, (t, LANES), 1) < HEAD_DIM
    for c in range(tm // t):
        rows = slice(c * t, (c + 1) * t)
        for p in range(GMLP_WIDTH // LANES):
            cols = slice(p * LANES, (p + 1) * LANES)
            vp = vn[rows, cols]
            mixed = jnp.where(first_head, _dot(ws[2 * p], vp), _dot(ws[2 * p + 1], vp))
            o_ref[rows, cols] = (u[rows, cols] * (mixed + bs_ref[:, cols])).astype(o_ref.dtype)


def _gmlp(uv, ln_w, ln_b, ws, bs, tm):
    m = uv.shape[0]
    w = GMLP_WIDTH
    bs_full = jnp.repeat(bs.T, HEAD_DIM, axis=1)
    return pl.pallas_call(
        functools.partial(_gmlp_kernel, tm=tm),
        grid=(m // tm,),
        in_specs=[pl.BlockSpec((tm, 2 * w), lambda i: (i, 0)),
                  pl.BlockSpec((1, w), lambda i: (0, 0)),
                  pl.BlockSpec((1, w), lambda i: (0, 0)),
                  pl.BlockSpec((GMLP_GROUPS, GMLP_CHUNK, GMLP_CHUNK), lambda i: (0, 0, 0)),
                  pl.BlockSpec((GMLP_CHUNK, w), lambda i: (0, 0))],
        out_specs=pl.BlockSpec((tm, w), lambda i: (i, 0)),
        out_shape=jax.ShapeDtypeStruct((m, w), BF16),
        compiler_params=_cparams(("parallel",)),
        name="gmlp",
    )(uv, ln_w.reshape(1, w), ln_b.reshape(1, w), ws, bs_full)


def _fcum_kernel(f_ref, b_ref, o_ref, *, seq):
    t = LANES
    r = lax.broadcasted_iota(jnp.int32, (LANES, LANES), 0)
    c = lax.broadcasted_iota(jnp.int32, (LANES, LANES), 1)
    spread = [jnp.where((c == 3 * r + part) & (r < FOX_HEADS), 1.0, 0.0).astype(BF16) for part in range(3)]

    def body(i, carry):
        r0 = pl.multiple_of(i * t, t)
        z = f_ref[pl.ds(r0, t), :] + b_ref[...]
        logf = jnp.minimum(z, 0.0) - jnp.log(1.0 + jnp.exp(-jnp.abs(z)))
        cs = _tri_cumsum(logf) + carry
        hi, mid, lo = _split3(-cs)
        o_ref[pl.ds(r0, t), :] = (_dot(hi, spread[0]) + _dot(mid, spread[1]) + _dot(lo, spread[2])).astype(o_ref.dtype)
        return cs[t - 1:t, :]

    lax.fori_loop(0, seq // t, body, jnp.zeros((1, LANES), F32))


def _forget_cumsum(proj, col_block, f_bias, seq):
    m = proj.shape[0]
    fb = jnp.pad(f_bias, (0, LANES - FOX_HEADS)).reshape(1, LANES)
    return pl.pallas_call(
        functools.partial(_fcum_kernel, seq=seq),
        grid=(m // seq,),
        in_specs=[pl.BlockSpec((seq, LANES), lambda b: (b, col_block)),
                  pl.BlockSpec((1, LANES), lambda b: (0, 0))],
        out_specs=pl.BlockSpec((seq, LANES), lambda b: (b, 0)),
        out_shape=jax.ShapeDtypeStruct((m, LANES), BF16),
        compiler_params=_cparams(("parallel",)),
        name="fox_cumsum",
    )(proj, fb)


def _attn_kernel(*refs, tq, tk, nblk, mode):
    if mode == "fox":
        q_ref, k_ref, ka_ref, vt_ref, o_ref, vp_scr, acc_scr, st_scr, p_scr, qc_scr = refs
    else:
        q_ref, ns_ref, k_ref, vt_ref, o_ref, vp_scr, acc_scr, st_scr, p_scr, qc_scr = refs
    p = pl.program_id(1)
    qi = pl.program_id(2)
    half = LANES // 2
    n_sub = tq // tk

    @pl.when(qi == 0)
    def _():
        chan = lax.broadcasted_iota(jnp.int32, (LANES, tk), 0)
        for jb in range(nblk):
            vt = vt_ref[:, jb * tk:(jb + 1) * tk]
            ones = jnp.ones_like(vt)
            vp_scr[0, jb] = jnp.where(chan < half, vt, ones)
            vp_scr[1, jb] = jnp.where(chan >= half, vt, ones)

    lane_k = lax.broadcasted_iota(jnp.int32, (tk, LANES), 1)
    q = q_ref[...]
    if mode == "fox":
        q = q * (HEAD_DIM ** -0.5)
    slots = LANES // MOBA_HEADS
    q_t = q.astype(F32).T
    chan_q = lax.broadcasted_iota(jnp.int32, (LANES, tq), 0)
    for h in range(2):
        head_rows = (chan_q < half) if h == 0 else (chan_q >= half)
        qh_t = jnp.where(head_rows, q_t, 0.0)
        if mode == "fox":
            first = 3 * (2 * p + h)
            aux_t = jnp.where((chan_q >= first) & (chan_q < first + 3), 1.0, 0.0)
        else:
            aux_t = ns_ref[...].astype(F32).T
        qc_scr[h] = jnp.concatenate([qh_t, aux_t], axis=0).astype(BF16)

    def scores(j):
        r0 = pl.multiple_of(j * tk, tk)
        kb = k_ref[pl.ds(r0, tk), :]
        out = []
        for h in range(2):
            if mode == "fox":
                ka = ka_ref[pl.ds(r0, tk), :]
            else:
                ka = jnp.where(lane_k == (2 * p + h) * slots + j, NEG_BIG, 0.0).astype(BF16)
            out.append(_dot(jnp.concatenate([kb, ka], axis=1), qc_scr[h]))
        return out

    def advance(j, ms, mask, prefetch):
        nxt = scores(j + 1) if prefetch else None
        prev = jnp.maximum(j - 1, 0)
        pv = [_dot(vp_scr[h, prev], p_scr[h]) for h in range(2)]
        ms2 = []
        for h in range(2):
            st = st_scr[h]
            if mask is not None:
                st = jnp.where(mask, st, NEG_BIG)
            m_new = jnp.maximum(ms[h], jnp.max(st, axis=0, keepdims=True))
            alpha = jnp.exp(ms[h] - m_new)
            pt = jnp.exp(st - m_new).astype(BF16)
            acc_scr[h] = alpha * (acc_scr[h] + pv[h])
            p_scr[h] = pt
            ms2.append(m_new)
        if prefetch:
            for h in range(2):
                st_scr[h] = nxt[h]
        return tuple(ms2)

    acc_scr[...] = jnp.zeros(acc_scr.shape, F32)
    p_scr[...] = jnp.zeros(p_scr.shape, BF16)
    first_scores = scores(0)
    for h in range(2):
        st_scr[h] = first_scores[h]
    m0 = jnp.full((1, tq), -jnp.inf, F32)
    first_diag = qi * n_sub

    def body(i, ms):
        for s in range(n_sub):
            ms = advance(i * n_sub + s, ms, None, True)
        return ms

    ms = lax.fori_loop(0, qi, body, (m0, m0))
    key_i = lax.broadcasted_iota(jnp.int32, (tk, tq), 0)
    qry_i = lax.broadcasted_iota(jnp.int32, (tk, tq), 1)
    for s in range(n_sub):
        ms = advance(first_diag + s, ms, (key_i + s * tk) <= qry_i, s + 1 < n_sub)
    last = first_diag + n_sub - 1
    outs = []
    for h in range(2):
        a = acc_scr[h] + _dot(vp_scr[h, last], p_scr[h])
        denom = a[(1 - h) * half:(1 - h) * half + 1, :]
        outs.append(a[h * half:(h + 1) * half, :] / denom)
    o_ref[...] = jnp.concatenate(outs, axis=0).T.astype(o_ref.dtype)


def _block_attention(mode, q_arr, q_col, aux_arr, k_arr, k_col, vt_arr, vt_row, seq, tq, tk):
    m = q_arr.shape[0]
    nb = m // seq
    nq = seq // tq
    pairs = FOX_HEADS // 2
    q_spec = pl.BlockSpec((tq, LANES), lambda b, p, i: (b * nq + i, q_col + p))
    k_spec = pl.BlockSpec((seq, LANES), lambda b, p, i: (b, k_col + p))
    vt_spec = pl.BlockSpec((LANES, seq), lambda b, p, i: (vt_row + p, b))
    if mode == "fox":
        in_specs = [q_spec, k_spec, pl.BlockSpec((seq, LANES), lambda b, p, i: (b, 0)), vt_spec]
        args = (q_arr, k_arr, aux_arr, vt_arr)
    else:
        in_specs = [q_spec, pl.BlockSpec((tq, LANES), lambda b, p, i: (b * nq + i, 0)), k_spec, vt_spec]
        args = (q_arr, aux_arr, k_arr, vt_arr)
    return pl.pallas_call(
        functools.partial(_attn_kernel, tq=tq, tk=tk, nblk=seq // tk, mode=mode),
        grid=(nb, pairs, nq),
        in_specs=in_specs,
        out_specs=pl.BlockSpec((tq, LANES), lambda b, p, i: (b * nq + i, p)),
        out_shape=jax.ShapeDtypeStruct((m, pairs * LANES), BF16),
        scratch_shapes=[pltpu.VMEM((2, seq // tk, LANES, tk), BF16), pltpu.VMEM((2, LANES, tq), F32),
                        pltpu.VMEM((2, tk, tq), F32), pltpu.VMEM((2, tk, tq), BF16),
                        pltpu.VMEM((2, 2 * LANES, tq), BF16)],
        compiler_params=_cparams(("arbitrary", "arbitrary", "arbitrary")),
        name=mode + "_attn",
    )(*args)


def _rope_tables(seq):
    half = ROPE_DIM // 2
    inv_freq = ROPE_THETA ** (-jnp.arange(half, dtype=F32) / half)
    ang = jnp.arange(seq, dtype=F32)[:, None] * inv_freq[None, :]
    cos, sin = jnp.cos(ang), jnp.sin(ang)
    ones = jnp.ones((seq, HEAD_DIM - ROPE_DIM), F32)
    cos_h = jnp.concatenate([cos, cos, ones], axis=1)
    sin_h = jnp.concatenate([-sin, sin, 0.0 * ones], axis=1)
    return jnp.tile(cos_h, (1, 2)), jnp.tile(sin_h, (1, 2))


def _rotary(x, cos_t, sin_t):
    half = ROPE_DIM // 2
    lane = lax.broadcasted_iota(jnp.int32, (x.shape[0], LANES), 1)
    first_half = (lane & (HEAD_DIM - 1)) < half
    outs = []
    for j in range(x.shape[1] // LANES):
        xc = x[:, j * LANES:(j + 1) * LANES]
        up = pltpu.roll(xc, LANES - half, axis=1)
        down = pltpu.roll(xc, half, axis=1)
        outs.append(xc * cos_t + jnp.where(first_half, up, down) * sin_t)
    return jnp.concatenate(outs, axis=1)


def _moba_k_kernel(k_ref, cos_ref, sin_ref, ko_ref, km_ref):
    kr = _rotary(k_ref[...], cos_ref[...], sin_ref[...])
    ko_ref[...] = kr.astype(ko_ref.dtype)
    km_ref[0] = jnp.mean(kr, axis=0, keepdims=True)


def _moba_k_prep(proj, k_col, cos_t, sin_t, seq):
    m = proj.shape[0]
    t = MOBA_BLOCK
    nblk = seq // t
    return pl.pallas_call(
        _moba_k_kernel,
        grid=(m // t,),
        in_specs=[pl.BlockSpec((t, MOBA_WIDTH), lambda i: (i, k_col)),
                  pl.BlockSpec((t, LANES), lambda i: (i % nblk, 0)),
                  pl.BlockSpec((t, LANES), lambda i: (i % nblk, 0))],
        out_specs=[pl.BlockSpec((t, MOBA_WIDTH), lambda i: (i, 0)),
                   pl.BlockSpec((1, 1, MOBA_WIDTH), lambda i: (i, 0, 0))],
        out_shape=[jax.ShapeDtypeStruct((m, MOBA_WIDTH), BF16),
                   jax.ShapeDtypeStruct((m // t, 1, MOBA_WIDTH), F32)],
        compiler_params=_cparams(("parallel",)),
        name="moba_k_prep",
    )(proj, cos_t, sin_t)


def _moba_q_kernel(q_ref, cos_ref, sin_ref, km_ref, qo_ref, ns_ref, *, nblk):
    own = pl.program_id(0) % nblk
    qr = _rotary(q_ref[...], cos_ref[...], sin_ref[...])
    qo_ref[...] = (qr * (HEAD_DIM ** -0.5)).astype(qo_ref.dtype)
    gate_t = _dot_f32_nt(km_ref[0], qr)
    slots = LANES // MOBA_HEADS
    n = lax.broadcasted_iota(jnp.int32, (slots, gate_t.shape[1]), 0)
    flags = []
    for h in range(MOBA_HEADS):
        g = gate_t[h * slots:(h + 1) * slots, :]
        cnt = jnp.zeros(g.shape, jnp.int32)
        for cand in range(slots):
            other = g[cand:cand + 1, :]
            beats = (other > g) | ((other == g) & (n > cand))
            cnt = cnt + jnp.where(beats, jnp.where(cand < own, 1, 0), 0)
        selected = (n == own) | ((n < own) & (cnt < MOBA_TOPK))
        flags.append(jnp.where(selected, 0.0, 1.0))
    ns_ref[...] = jnp.concatenate(flags, axis=0).T.astype(ns_ref.dtype)


def _moba_q_prep(proj, q_col, cos_t, sin_t, km_mat, seq):
    m = proj.shape[0]
    t = MOBA_BLOCK
    nblk = seq // t
    return pl.pallas_call(
        functools.partial(_moba_q_kernel, nblk=nblk),
        grid=(m // t,),
        in_specs=[pl.BlockSpec((t, MOBA_WIDTH), lambda i: (i, q_col)),
                  pl.BlockSpec((t, LANES), lambda i: (i % nblk, 0)),
                  pl.BlockSpec((t, LANES), lambda i: (i % nblk, 0)),
                  pl.BlockSpec((1, LANES, MOBA_WIDTH), lambda i: (i // nblk, 0, 0))],
        out_specs=[pl.BlockSpec((t, MOBA_WIDTH), lambda i: (i, 0)),
                   pl.BlockSpec((t, LANES), lambda i: (i, 0))],
        out_shape=[jax.ShapeDtypeStruct((m, MOBA_WIDTH), BF16),
                   jax.ShapeDtypeStruct((m, LANES), BF16)],
        compiler_params=_cparams(("parallel",)),
        name="moba_q_prep",
    )(proj, cos_t, sin_t, km_mat)


def _pair_cols(x, h0, h1, shape):
    lane = lax.broadcasted_iota(jnp.int32, shape, 1)
    a = jnp.broadcast_to(x[:, h0:h0 + 1], shape)
    b = jnp.broadcast_to(x[:, h1:h1 + 1], shape)
    return jnp.where(lane < SSM_HEAD_DIM, a, b)


def _ssd_kernel(in_ref, cw_ref, cb_ref, dtb_ref, alog_ref, dsk_ref, nw_ref, o_ref, xpad_scr, st_scr):
    t = SSM_CHUNK
    c = pl.program_id(1)
    halo = SUBLANES

    @pl.when(c == 0)
    def _():
        xpad_scr[0:halo, :] = jnp.zeros((halo, SSM_CONV_DIM), F32)
        st_scr[...] = jnp.zeros(st_scr.shape, F32)

    xpad_scr[halo:halo + t, :] = in_ref[:, 0:SSM_CONV_DIM]
    conv = cb_ref[...] + cw_ref[0:1, :] * xpad_scr[pl.ds(halo - SSM_CONV + 1, t), :]
    for k in range(1, SSM_CONV):
        conv = conv + cw_ref[k:k + 1, :] * xpad_scr[pl.ds(halo - SSM_CONV + 1 + k, t), :]
    xpad_scr[0:halo, :] = xpad_scr[t:t + halo, :]
    xbc = conv * _sigmoid(conv)
    z = in_ref[:, SSM_CONV_DIM:SSM_CONV_DIM + SSM_WIDTH]
    dt = _softplus(in_ref[:, SSM_CONV_DIM + SSM_WIDTH:] + dtb_ref[...])
    a = -jnp.exp(alog_ref[...])
    acum = _tri_cumsum(dt * a)
    acum_t = acum.T
    tot = acum[t - 1:t, :]
    decay_end = jnp.exp(tot - acum)
    decay_in = jnp.exp(acum)
    exp_tot = jnp.exp(tot)
    row = lax.broadcasted_iota(jnp.int32, (t, t), 0)
    col = lax.broadcasted_iota(jnp.int32, (t, t), 1)
    causal = row >= col
    lane = lax.broadcasted_iota(jnp.int32, (t, LANES), 1)
    lane1 = lax.broadcasted_iota(jnp.int32, (1, LANES), 1)
    pair_shape = (t, LANES)
    rep = SSM_HEADS // SSM_GROUPS
    b_off = SSM_WIDTH
    c_off = SSM_WIDTH + SSM_GROUPS * SSM_STATE
    ys = []
    for g in range(SSM_GROUPS):
        bm = xbc[:, b_off + g * SSM_STATE:b_off + (g + 1) * SSM_STATE]
        cm = xbc[:, c_off + g * SSM_STATE:c_off + (g + 1) * SSM_STATE].astype(BF16)
        cb = _dot_nt(cm, bm.astype(BF16))
        bm_t = bm.T.astype(BF16)
        for pr in range(rep // 2):
            j = g * (rep // 2) + pr
            h0, h1 = 2 * j, 2 * j + 1
            xp = xbc[:, j * LANES:(j + 1) * LANES]
            xdt = xp * _pair_cols(dt, h0, h1, pair_shape)
            xdt_b = xdt.astype(BF16)
            yd = []
            for h in (h0, h1):
                diff = acum[:, h:h + 1] - acum_t[h:h + 1, :]
                w = cb * jnp.exp(jnp.where(causal, diff, NEG_BIG))
                yd.append(_dot(w.astype(BF16), xdt_b))
            y_diag = jnp.where(lane < SSM_HEAD_DIM, yd[0], yd[1])
            prev = st_scr[j]
            y_off = _dot(cm, prev.astype(BF16)) * _pair_cols(decay_in, h0, h1, pair_shape)
            xw = (xdt * _pair_cols(decay_end, h0, h1, pair_shape)).astype(BF16)
            scale = jnp.where(lane1 < SSM_HEAD_DIM, exp_tot[:, h0:h0 + 1], exp_tot[:, h1:h1 + 1])
            st_scr[j] = prev * scale + _dot(bm_t, xw)
            ys.append(y_diag + y_off + dsk_ref[:, j * LANES:(j + 1) * LANES] * xp)
    y = jnp.concatenate(ys, axis=1)
    y = y * (z * _sigmoid(z))
    gw = SSM_WIDTH // SSM_GROUPS
    outs = []
    for g in range(SSM_GROUPS):
        seg = y[:, g * gw:(g + 1) * gw]
        ms = jnp.mean(seg * seg, axis=-1, keepdims=True)
        outs.append(seg * lax.rsqrt(ms + NORM_EPS) * nw_ref[:, g * gw:(g + 1) * gw])
    o_ref[...] = jnp.concatenate(outs, axis=1).astype(o_ref.dtype)


def _ssd(proj, conv_w, conv_b, dt_bias, a_log, d_skip, norm_w, seq):
    m, wtot = proj.shape
    t = SSM_CHUNK
    nc = seq // t
    pad = LANES - SSM_HEADS
    dtb = jnp.pad(dt_bias, (0, pad)).reshape(1, LANES)
    alog = jnp.pad(a_log, (0, pad)).reshape(1, LANES)
    dsk = jnp.repeat(d_skip, SSM_HEAD_DIM).reshape(1, SSM_WIDTH)
    const = lambda b, c: (0, 0)
    return pl.pallas_call(
        _ssd_kernel,
        grid=(m // seq, nc),
        in_specs=[pl.BlockSpec((t, wtot), lambda b, c: (b * nc + c, 0)),
                  pl.BlockSpec((SSM_CONV, SSM_CONV_DIM), const),
                  pl.BlockSpec((1, SSM_CONV_DIM), const),
                  pl.BlockSpec((1, LANES), const),
                  pl.BlockSpec((1, LANES), const),
                  pl.BlockSpec((1, SSM_WIDTH), const),
                  pl.BlockSpec((1, SSM_WIDTH), const)],
        out_specs=pl.BlockSpec((t, SSM_WIDTH), lambda b, c: (b * nc + c, 0)),
        out_shape=jax.ShapeDtypeStruct((m, SSM_WIDTH), BF16),
        scratch_shapes=[pltpu.VMEM((t + SUBLANES, SSM_CONV_DIM), F32),
                        pltpu.VMEM((SSM_HEADS // 2, SSM_STATE, LANES), F32)],
        compiler_params=_cparams(("parallel", "arbitrary")),
        name="ssd",
    )(proj, conv_w, conv_b.reshape(1, SSM_CONV_DIM), dtb, alog, dsk, norm_w.reshape(1, SSM_WIDTH))


def _merge_kernel(ya_ref, yb_ref, yc_ref, yd_ref, g_ref, x_ref, g1_ref,
                  wa_ref, wb_ref, wc_ref, wd_ref, wo_ref, o_ref):
    d = x_ref.shape[1]
    gate = lambda i: _sigmoid(g_ref[:, i * d:(i + 1) * d].astype(F32))
    merged = gate(0) * _dot(ya_ref[...], wa_ref[...])
    merged = merged + gate(1) * _dot(yb_ref[...], wb_ref[...])
    merged = merged + gate(2) * _dot(yc_ref[...], wc_ref[...])
    merged = merged + gate(3) * _dot(yd_ref[...], wd_ref[...])
    o_ref[...] = x_ref[...] + g1_ref[0] * _dot(merged.astype(BF16), wo_ref[...])


def _merge(ya, yb, yc, yd, gates, x, mod, k_gate, wa, wb, wc, wd, wo, seq, tm):
    m, d = x.shape
    nb = m // seq
    rows = lambda i: (i, 0)
    const = lambda i: (0, 0)
    return pl.pallas_call(
        _merge_kernel,
        grid=(m // tm,),
        in_specs=[pl.BlockSpec((tm, ya.shape[1]), rows),
                  pl.BlockSpec((tm, yb.shape[1]), rows),
                  pl.BlockSpec((tm, yc.shape[1]), rows),
                  pl.BlockSpec((tm, yd.shape[1]), rows),
                  pl.BlockSpec((tm, N_BRANCH * d), rows),
                  pl.BlockSpec((tm, d), rows),
                  pl.BlockSpec((1, 1, d), lambda i: (k_gate * nb + (i * tm) // seq, 0, 0)),
                  pl.BlockSpec(wa.shape, const),
                  pl.BlockSpec(wb.shape, const),
                  pl.BlockSpec(wc.shape, const),
                  pl.BlockSpec(wd.shape, const),
                  pl.BlockSpec(wo.shape, const)],
        out_specs=pl.BlockSpec((tm, d), rows),
        out_shape=jax.ShapeDtypeStruct((m, d), F32),
        compiler_params=_cparams(("parallel",)),
        name="merge",
    )(ya, yb, yc, yd, gates, x, mod, wa, wb, wc, wd, wo)


def _ffn_kernel(x_ref, nw_ref, sc_ref, sh_ref, g_ref, w1_ref, w2_ref, o_ref, h_scr, acc_scr):
    j = pl.program_id(1)

    @pl.when(j == 0)
    def _():
        h_scr[...] = _rms_mod(x_ref[...], nw_ref[...], sc_ref[0], sh_ref[0]).astype(BF16)
        acc_scr[...] = jnp.zeros(acc_scr.shape, F32)

    a = jnp.maximum(_dot(h_scr[...], w1_ref[...]), 0.0)
    acc_scr[...] += _dot((a * a).astype(BF16), w2_ref[...])

    @pl.when(j == pl.num_programs(1) - 1)
    def _():
        o_ref[...] = x_ref[...] + g_ref[0] * acc_scr[...]


def _ffn(x, nw, mod, k_sc, k_sh, k_gate, w1, w2, seq, tm, tf):
    m, d = x.shape
    nb = m // seq
    dff = w1.shape[1]
    modrow = lambda k: (lambda i, j: (k * nb + (i * tm) // seq, 0, 0))
    return pl.pallas_call(
        _ffn_kernel,
        grid=(m // tm, dff // tf),
        in_specs=[pl.BlockSpec((tm, d), lambda i, j: (i, 0)),
                  pl.BlockSpec((1, d), lambda i, j: (0, 0)),
                  pl.BlockSpec((1, 1, d), modrow(k_sc)),
                  pl.BlockSpec((1, 1, d), modrow(k_sh)),
                  pl.BlockSpec((1, 1, d), modrow(k_gate)),
                  pl.BlockSpec((d, tf), lambda i, j: (0, j)),
                  pl.BlockSpec((tf, d), lambda i, j: (j, 0))],
        out_specs=pl.BlockSpec((tm, d), lambda i, j: (i, 0)),
        out_shape=jax.ShapeDtypeStruct((m, d), F32),
        scratch_shapes=[pltpu.VMEM((tm, d), BF16), pltpu.VMEM((tm, d), F32)],
        compiler_params=_cparams(("parallel", "arbitrary"), FFN_VMEM_LIMIT),
        name="ffn",
    )(x, nw.reshape(1, d), mod, mod, mod, w1, w2)


def _pad_rows(w, rows):
    return jnp.pad(w, ((0, rows - w.shape[0]), (0, 0)))


def _split_w_in(w_in, l):
    wt = jnp.transpose(w_in, (2, 0, 1))[:, l, :]
    o = 0
    uv = wt[o:o + 2 * GMLP_WIDTH]; o += 2 * GMLP_WIDTH
    fox_qk = wt[o:o + 2 * FOX_WIDTH]; o += 2 * FOX_WIDTH
    fox_v = wt[o:o + FOX_WIDTH]; o += FOX_WIDTH
    fox_f = wt[o:o + FOX_HEADS]; o += FOX_HEADS
    moba_qk = wt[o:o + 2 * MOBA_WIDTH]; o += 2 * MOBA_WIDTH
    moba_v = wt[o:o + MOBA_WIDTH]; o += MOBA_WIDTH
    z = wt[o:o + SSM_WIDTH]; o += SSM_WIDTH
    xbc = wt[o:o + SSM_CONV_DIM]; o += SSM_CONV_DIM
    dt = wt[o:o + SSM_HEADS]; o += SSM_HEADS
    gates = wt[o:]
    w_v = jnp.concatenate([fox_v, moba_v], axis=0)
    w_mqkf = jnp.concatenate([moba_qk, _pad_rows(fox_f, LANES)], axis=0)
    w_ssd = jnp.concatenate([xbc, z, _pad_rows(dt, LANES)], axis=0)
    return [g.astype(BF16) for g in (uv, fox_qk, w_v, w_mqkf, w_ssd, gates)]


def _tile_rows(seq, want):
    return min(want, seq)


def kernel(x, c, ada_w, ada_b, norm_mix_w, w_in, gmlp_ln_w, gmlp_ln_b, gmlp_ws, gmlp_bs, fox_f_bias, ssm_conv_w, ssm_conv_b, ssm_dt_bias, ssm_a_log, ssm_d, ssm_norm_w, w_branch_a, w_branch_b, w_branch_c, w_branch_d, w_out, norm_mlp_w, mlp_w1, mlp_w2, final_norm_w):
    nb, seq, d = x.shape
    m = nb * seq
    depth = ada_w.shape[0]
    tm = _tile_rows(seq, 512)
    tm_proj = _tile_rows(seq, 1024)
    tq_attn = _tile_rows(seq, ATTN_Q_TILE)
    pairs = FOX_HEADS // 2
    nblk = seq // MOBA_BLOCK
    slots = LANES // MOBA_HEADS
    assert seq % MOBA_BLOCK == 0 and nblk <= slots and seq % SSM_CHUNK == 0

    mod_all = _modulation(c, ada_w, ada_b)[:, :nb]
    mod_all = mod_all.reshape(depth, nb, 6, d).transpose(0, 2, 1, 3).reshape(depth, 6 * nb, 1, d)
    cos_t, sin_t = _rope_tables(seq)
    head_eye = jnp.eye(MOBA_HEADS, dtype=F32)

    xf = x.reshape(m, d)
    for l in range(depth):
        mod = mod_all[l]
        w_uv, w_fqk, w_v, w_mqkf, w_ssd, w_gate = _split_w_in(w_in, l)
        h = _norm_mod(xf, norm_mix_w[l], mod, 1, 0, seq, tm)
        p_uv = _matmul(h, w_uv, F32, tm_proj, w_uv.shape[0], "proj_uv")
        p_fqk = _matmul(h, w_fqk, BF16, tm_proj, w_fqk.shape[0], "proj_fox_qk")
        p_vt = _matmul_t(h, w_v, BF16, tm_proj, "proj_vt")
        p_mqkf = _matmul(h, w_mqkf, F32, tm_proj, w_mqkf.shape[0], "proj_mqkf")
        p_ssd = _matmul(h, w_ssd, F32, tm_proj, w_ssd.shape[0], "proj_ssd")
        p_gate = _matmul(h, w_gate, BF16, tm_proj, w_gate.shape[0], "proj_gate")

        y_a = _gmlp(p_uv, gmlp_ln_w[l], gmlp_ln_b[l], gmlp_ws[l], gmlp_bs[l], tm)

        k_aux = _forget_cumsum(p_mqkf, 2 * MOBA_WIDTH // LANES, fox_f_bias[l], seq)
        y_b = _block_attention("fox", p_fqk, 0, k_aux, p_fqk, pairs, p_vt, 0, seq, tq_attn, MOBA_BLOCK)

        k_rot, k_mean = _moba_k_prep(p_mqkf, 1, cos_t, sin_t, seq)
        km = jnp.pad(k_mean.reshape(nb, nblk, MOBA_HEADS, HEAD_DIM), ((0, 0), (0, slots - nblk), (0, 0), (0, 0)))
        km_mat = jnp.einsum('bnhd,kh->bknhd', km, head_eye).reshape(nb, LANES, MOBA_WIDTH)
        q_rot, not_sel = _moba_q_prep(p_mqkf, 0, cos_t, sin_t, km_mat, seq)
        y_c = _block_attention("moba", q_rot, 0, not_sel, k_rot, 0, p_vt, pairs, seq, tq_attn, MOBA_BLOCK)

        y_d = _ssd(p_ssd, ssm_conv_w[l], ssm_conv_b[l], ssm_dt_bias[l], ssm_a_log[l], ssm_d[l],
                   ssm_norm_w[l], seq)

        xf = _merge(y_a, y_b, y_c, y_d, p_gate, xf, mod, 2,
                    w_branch_a[l].astype(BF16), w_branch_b[l].astype(BF16), w_branch_c[l].astype(BF16),
                    w_branch_d[l].astype(BF16), w_out[l].astype(BF16), seq, tm)
        xf = _ffn(xf, norm_mlp_w[l], mod, 4, 3, 5, mlp_w1[l].astype(BF16), mlp_w2[l].astype(BF16),
                  seq, tm, mlp_w1.shape[2])
    return _final_norm(xf, final_norm_w, tm).reshape(nb, seq, d)
```

```python
import functools
import math

import jax
import jax.numpy as jnp
from jax import lax
from jax.experimental import pallas as pl
from jax.experimental.pallas import tpu as pltpu

F32 = jnp.float32
BF16 = jnp.bfloat16

HEAD_DIM = 64
NORM_EPS = 1e-6
GMLP_GROUPS = 8
GMLP_WIDTH = GMLP_GROUPS * HEAD_DIM
GMLP_CHUNK = 128
FOX_HEADS = 8
FOX_WIDTH = FOX_HEADS * HEAD_DIM
MOBA_HEADS = 8
MOBA_WIDTH = MOBA_HEADS * HEAD_DIM
MOBA_BLOCK = 256
MOBA_TOPK = 3
ROPE_THETA = 500000.0
ROPE_DIM = HEAD_DIM // 4
SSM_HEADS = 12
SSM_HEAD_DIM = 64
SSM_WIDTH = SSM_HEADS * SSM_HEAD_DIM
SSM_GROUPS = 2
SSM_STATE = 128
SSM_CONV = 4
SSM_CHUNK = 128
SSM_CONV_DIM = SSM_WIDTH + 2 * SSM_GROUPS * SSM_STATE
N_BRANCH = 4

LANES = 128
SUBLANES = 8
NEG_BIG = -1e30
VMEM_LIMIT = 48 * 1024 * 1024
FFN_VMEM_LIMIT = 56 * 1024 * 1024
ATTN_Q_TILE = 512


def _cparams(sem, vmem_limit=VMEM_LIMIT):
    return pltpu.CompilerParams(dimension_semantics=sem, vmem_limit_bytes=vmem_limit)


def _sigmoid(x):
    return 1.0 / (1.0 + jnp.exp(-x))


def _softplus(x):
    return jnp.maximum(x, 0.0) + jnp.log(1.0 + jnp.exp(-jnp.abs(x)))


def _dot(a, b):
    return jnp.dot(a, b, preferred_element_type=F32)


def _dot_nt(a, b):
    return lax.dot_general(a, b, (((1,), (1,)), ((), ())), preferred_element_type=F32)


def _keep_high_half(x):
    bits = lax.bitcast_convert_type(x, jnp.uint32) & jnp.uint32(0xFFFF0000)
    return lax.bitcast_convert_type(bits, F32)


def _split3(x):
    hi = _keep_high_half(x)
    r = x - hi
    mid = _keep_high_half(r)
    lo = r - mid
    return hi.astype(BF16), mid.astype(BF16), lo.astype(BF16)


def _tri_cumsum(x):
    t = x.shape[0]
    row = lax.broadcasted_iota(jnp.int32, (t, t), 0)
    col = lax.broadcasted_iota(jnp.int32, (t, t), 1)
    tri = jnp.where(row >= col, 1.0, 0.0).astype(BF16)
    hi, mid, lo = _split3(x)
    return _dot(tri, hi) + _dot(tri, mid) + _dot(tri, lo)


def _six_pass(dot, a, b):
    ah, am, al = _split3(a)
    bh, bm, bl = _split3(b)
    small = dot(am, bm) + dot(ah, bl) + dot(al, bh)
    return dot(ah, bh) + (dot(ah, bm) + dot(am, bh)) + small


def _dot_f32(a, b):
    return _six_pass(_dot, a, b)


def _dot_f32_nt(a, b):
    return _six_pass(_dot_nt, a, b)


def _mod_kernel(c_ref, w_ref, b_ref, o_ref):
    c = c_ref[...]
    ca = c * _sigmoid(c)
    o_ref[0] = _dot_f32(ca, w_ref[0]) + b_ref[0]


def _modulation(c, ada_w, ada_b):
    nb, d = c.shape
    nl = ada_w.shape[0]
    bp = -(-nb // SUBLANES) * SUBLANES
    cp = jnp.pad(c, ((0, bp - nb), (0, 0)))
    return pl.pallas_call(
        _mod_kernel,
        grid=(nl, 6),
        in_specs=[pl.BlockSpec((bp, d), lambda l, k: (0, 0)),
                  pl.BlockSpec((1, d, d), lambda l, k: (l, 0, k)),
                  pl.BlockSpec((1, 1, d), lambda l, k: (l, 0, k))],
        out_specs=pl.BlockSpec((1, bp, d), lambda l, k: (l, 0, k)),
        out_shape=jax.ShapeDtypeStruct((nl, bp, 6 * d), F32),
        compiler_params=_cparams(("parallel", "parallel")),
        name="adaln_mod",
    )(cp, ada_w, ada_b.reshape(nl, 1, 6 * d))


def _rms_mod(x, nw, sc, sh):
    ms = jnp.mean(x * x, axis=-1, keepdims=True)
    xn = x * lax.rsqrt(ms + NORM_EPS)
    return xn * nw * (1.0 + sc) + sh


def _norm_kernel(x_ref, nw_ref, sc_ref, sh_ref, o_ref):
    o_ref[...] = _rms_mod(x_ref[...], nw_ref[...], sc_ref[0], sh_ref[0]).astype(o_ref.dtype)


def _norm_mod(x, nw, mod, k_sc, k_sh, seq, tm):
    m, d = x.shape
    nb = m // seq
    return pl.pallas_call(
        _norm_kernel,
        grid=(m // tm,),
        in_specs=[pl.BlockSpec((tm, d), lambda i: (i, 0)),
                  pl.BlockSpec((1, d), lambda i: (0, 0)),
                  pl.BlockSpec((1, 1, d), lambda i: (k_sc * nb + (i * tm) // seq, 0, 0)),
                  pl.BlockSpec((1, 1, d), lambda i: (k_sh * nb + (i * tm) // seq, 0, 0))],
        out_specs=pl.BlockSpec((tm, d), lambda i: (i, 0)),
        out_shape=jax.ShapeDtypeStruct((m, d), BF16),
        compiler_params=_cparams(("parallel",)),
        name="norm_mod",
    )(x, nw.reshape(1, d), mod, mod)


def _final_norm_kernel(x_ref, nw_ref, o_ref):
    x = x_ref[...]
    ms = jnp.mean(x * x, axis=-1, keepdims=True)
    o_ref[...] = x * lax.rsqrt(ms + NORM_EPS) * nw_ref[...]


def _final_norm(x, nw, tm):
    m, d = x.shape
    return pl.pallas_call(
        _final_norm_kernel,
        grid=(m // tm,),
        in_specs=[pl.BlockSpec((tm, d), lambda i: (i, 0)),
                  pl.BlockSpec((1, d), lambda i: (0, 0))],
        out_specs=pl.BlockSpec((tm, d), lambda i: (i, 0)),
        out_shape=jax.ShapeDtypeStruct((m, d), F32),
        compiler_params=_cparams(("parallel",)),
        name="final_norm",
    )(x, nw.reshape(1, d))


def _mm_kernel(a_ref, wt_ref, o_ref):
    o_ref[...] = _dot_nt(a_ref[...], wt_ref[...]).astype(o_ref.dtype)


def _matmul(a, wt, out_dtype, tm, tn, name):
    m, k = a.shape
    n = wt.shape[0]
    return pl.pallas_call(
        _mm_kernel,
        grid=(n // tn, m // tm),
        in_specs=[pl.BlockSpec((tm, k), lambda j, i: (i, 0)),
                  pl.BlockSpec((tn, k), lambda j, i: (j, 0))],
        out_specs=pl.BlockSpec((tm, tn), lambda j, i: (i, j)),
        out_shape=jax.ShapeDtypeStruct((m, n), out_dtype),
        compiler_params=_cparams(("parallel", "parallel")),
        name=name,
    )(a, wt)


def _mm_t_kernel(a_ref, wt_ref, o_ref):
    o_ref[...] = _dot_nt(wt_ref[...], a_ref[...]).astype(o_ref.dtype)


def _matmul_t(a, wt, out_dtype, tm, name):
    m, k = a.shape
    n = wt.shape[0]
    return pl.pallas_call(
        _mm_t_kernel,
        grid=(m // tm,),
        in_specs=[pl.BlockSpec((tm, k), lambda i: (i, 0)),
                  pl.BlockSpec((n, k), lambda i: (0, 0))],
        out_specs=pl.BlockSpec((n, tm), lambda i: (0, i)),
        out_shape=jax.ShapeDtypeStruct((n, m), out_dtype),
        compiler_params=_cparams(("parallel",)),
        name=name,
    )(a, wt)


def _gelu_tanh(x):
    c = math.sqrt(2.0 / math.pi)
    return 0.5 * x * (1.0 + jnp.tanh(c * (x + 0.044715 * (x * x * x))))


def _gmlp_kernel(uv_ref, lnw_ref, lnb_ref, ws_ref, bs_ref, o_ref, *, tm):
    g = _gelu_tanh(uv_ref[...])
    u = g[:, :GMLP_WIDTH]
    v = g[:, GMLP_WIDTH:]
    mu = jnp.mean(v, axis=-1, keepdims=True)
    vc = v - mu
    var = jnp.mean(vc * vc, axis=-1, keepdims=True)
    vn = (vc * lax.rsqrt(var + NORM_EPS) * lnw_ref[...] + lnb_ref[...]).astype(BF16)
    t = GMLP_CHUNK
    row = lax.broadcasted_iota(jnp.int32, (t, t), 0)
    col = lax.broadcasted_iota(jnp.int32, (t, t), 1)
    causal = row >= col
    ws = [jnp.where(causal, ws_ref[i], 0.0).astype(BF16) for i in range(GMLP_GROUPS)]
    first_head = lax.broadcasted_iota(jnp.int32, (t, LANES), 1) < HEAD_DIM
    for c in range(tm // t):
        rows = slice(c * t, (c + 1) * t)
        for p in range(GMLP_WIDTH // LANES):
            cols = slice(p * LANES, (p + 1) * LANES)
            vp = vn[rows, cols]
            mixed = jnp.where(first_head, _dot(ws[2 * p], vp), _dot(ws[2 * p + 1], vp))
            o_ref[rows, cols] = (u[rows, cols] * (mixed + bs_ref[:, cols])).astype(o_ref.dtype)


def _gmlp(uv, ln_w, ln_b, ws, bs, tm):
    m = uv.shape[0]
    w = GMLP_WIDTH
    bs_full = jnp.repeat(bs.T, HEAD_DIM, axis=1)
    return pl.pallas_call(
        functools.partial(_gmlp_kernel, tm=tm),
        grid=(m // tm,),
        in_specs=[pl.BlockSpec((tm, 2 * w), lambda i: (i, 0)),
                  pl.BlockSpec((1, w), lambda i: (0, 0)),
                  pl.BlockSpec((1, w), lambda i: (0, 0)),
                  pl.BlockSpec((GMLP_GROUPS, GMLP_CHUNK, GMLP_CHUNK), lambda i: (0, 0, 0)),
                  pl.BlockSpec((GMLP_CHUNK, w), lambda i: (0, 0))],
        out_specs=pl.BlockSpec((tm, w), lambda i: (i, 0)),
        out_shape=jax.ShapeDtypeStruct((m, w), BF16),
        compiler_params=_cparams(("parallel",)),
        name="gmlp",
    )(uv, ln_w.reshape(1, w), ln_b.reshape(1, w), ws, bs_full)


def _fcum_kernel(f_ref, b_ref, o_ref, *, seq):
    t = LANES
    r = lax.broadcasted_iota(jnp.int32, (LANES, LANES), 0)
    c = lax.broadcasted_iota(jnp.int32, (LANES, LANES), 1)
    spread = [jnp.where((c == 3 * r + part) & (r < FOX_HEADS), 1.0, 0.0).astype(BF16) for part in range(3)]

    def body(i, carry):
        r0 = pl.multiple_of(i * t, t)
        z = f_ref[pl.ds(r0, t), :] + b_ref[...]
        logf = jnp.minimum(z, 0.0) - jnp.log(1.0 + jnp.exp(-jnp.abs(z)))
        cs = _tri_cumsum(logf) + carry
        hi, mid, lo = _split3(-cs)
        o_ref[pl.ds(r0, t), :] = (_dot(hi, spread[0]) + _dot(mid, spread[1]) + _dot(lo, spread[2])).astype(o_ref.dtype)
        return cs[t - 1:t, :]

    lax.fori_loop(0, seq // t, body, jnp.zeros((1, LANES), F32))


def _forget_cumsum(proj, col_block, f_bias, seq):
    m = proj.shape[0]
    fb = jnp.pad(f_bias, (0, LANES - FOX_HEADS)).reshape(1, LANES)
    return pl.pallas_call(
        functools.partial(_fcum_kernel, seq=seq),
        grid=(m // seq,),
        in_specs=[pl.BlockSpec((seq, LANES), lambda b: (b, col_block)),
                  pl.BlockSpec((1, LANES), lambda b: (0, 0))],
        out_specs=pl.BlockSpec((seq, LANES), lambda b: (b, 0)),
        out_shape=jax.ShapeDtypeStruct((m, LANES), BF16),
        compiler_params=_cparams(("parallel",)),
        name="fox_cumsum",
    )(proj, fb)


def _attn_kernel(*refs, tq, tk, nblk, mode):
    if mode == "fox":
        q_ref, k_ref, ka_ref, vt_ref, o_ref, vp_scr, acc_scr, st_scr, p_scr, qc_scr = refs
    else:
        q_ref, ns_ref, k_ref, vt_ref, o_ref, vp_scr, acc_scr, st_scr, p_scr, qc_scr = refs
    p = pl.program_id(1)
    qi = pl.program_id(2)
    half = LANES // 2
    n_sub = tq // tk

    @pl.when(qi == 0)
    def _():
        chan = lax.broadcasted_iota(jnp.int32, (LANES, tk), 0)
        for jb in range(nblk):
            vt = vt_ref[:, jb * tk:(jb + 1) * tk]
            ones = jnp.ones_like(vt)
            vp_scr[0, jb] = jnp.where(chan < half, vt, ones)
            vp_scr[1, jb] = jnp.where(chan >= half, vt, ones)

    lane_k = lax.broadcasted_iota(jnp.int32, (tk, LANES), 1)
    q = q_ref[...]
    if mode == "fox":
        q = q * (HEAD_DIM ** -0.5)
    slots = LANES // MOBA_HEADS
    q_t = q.astype(F32).T
    chan_q = lax.broadcasted_iota(jnp.int32, (LANES, tq), 0)
    for h in range(2):
        head_rows = (chan_q < half) if h == 0 else (chan_q >= half)
        qh_t = jnp.where(head_rows, q_t, 0.0)
        if mode == "fox":
            first = 3 * (2 * p + h)
            aux_t = jnp.where((chan_q >= first) & (chan_q < first + 3), 1.0, 0.0)
        else:
            aux_t = ns_ref[...].astype(F32).T
        qc_scr[h] = jnp.concatenate([qh_t, aux_t], axis=0).astype(BF16)

    def scores(j):
        r0 = pl.multiple_of(j * tk, tk)
        kb = k_ref[pl.ds(r0, tk), :]
        out = []
        for h in range(2):
            if mode == "fox":
                ka = ka_ref[pl.ds(r0, tk), :]
            else:
                ka = jnp.where(lane_k == (2 * p + h) * slots + j, NEG_BIG, 0.0).astype(BF16)
            out.append(_dot(jnp.concatenate([kb, ka], axis=1), qc_scr[h]))
        return out

    def advance(j, ms, mask, prefetch):
        nxt = scores(j + 1) if prefetch else None
        prev = jnp.maximum(j - 1, 0)
        pv = [_dot(vp_scr[h, prev], p_scr[h]) for h in range(2)]
        ms2 = []
        for h in range(2):
            st = st_scr[h]
            if mask is not None:
                st = jnp.where(mask, st, NEG_BIG)
            m_new = jnp.maximum(ms[h], jnp.max(st, axis=0, keepdims=True))
            alpha = jnp.exp(ms[h] - m_new)
            pt = jnp.exp(st - m_new).astype(BF16)
            acc_scr[h] = alpha * (acc_scr[h] + pv[h])
            p_scr[h] = pt
            ms2.append(m_new)
        if prefetch:
            for h in range(2):
                st_scr[h] = nxt[h]
        return tuple(ms2)

    acc_scr[...] = jnp.zeros(acc_scr.shape, F32)
    p_scr[...] = jnp.zeros(p_scr.shape, BF16)
    first_scores = scores(0)
    for h in range(2):
        st_scr[h] = first_scores[h]
    m0 = jnp.full((1, tq), -jnp.inf, F32)
    first_diag = qi * n_sub

    def body(i, ms):
        for s in range(n_sub):
            ms = advance(i * n_sub + s, ms, None, True)
        return ms

    ms = lax.fori_loop(0, qi, body, (m0, m0))
    key_i = lax.broadcasted_iota(jnp.int32, (tk, tq), 0)
    qry_i = lax.broadcasted_iota(jnp.int32, (tk, tq), 1)
    for s in range(n_sub):
        ms = advance(first_diag + s, ms, (key_i + s * tk) <= qry_i, s + 1 < n_sub)
    last = first_diag + n_sub - 1
    outs = []
    for h in range(2):
        a = acc_scr[h] + _dot(vp_scr[h, last], p_scr[h])
        denom = a[(1 - h) * half:(1 - h) * half + 1, :]
        outs.append(a[h * half:(h + 1) * half, :] / denom)
    o_ref[...] = jnp.concatenate(outs, axis=0).T.astype(o_ref.dtype)


def _block_attention(mode, q_arr, q_col, aux_arr, k_arr, k_col, vt_arr, vt_row, seq, tq, tk):
    m = q_arr.shape[0]
    nb = m // seq
    nq = seq // tq
    pairs = FOX_HEADS // 2
    q_spec = pl.BlockSpec((tq, LANES), lambda b, p, i: (b * nq + i, q_col + p))
    k_spec = pl.BlockSpec((seq, LANES), lambda b, p, i: (b, k_col + p))
    vt_spec = pl.BlockSpec((LANES, seq), lambda b, p, i: (vt_row + p, b))
    if mode == "fox":
        in_specs = [q_spec, k_spec, pl.BlockSpec((seq, LANES), lambda b, p, i: (b, 0)), vt_spec]
        args = (q_arr, k_arr, aux_arr, vt_arr)
    else:
        in_specs = [q_spec, pl.BlockSpec((tq, LANES), lambda b, p, i: (b * nq + i, 0)), k_spec, vt_spec]
        args = (q_arr, aux_arr, k_arr, vt_arr)
    return pl.pallas_call(
        functools.partial(_attn_kernel, tq=tq, tk=tk, nblk=seq // tk, mode=mode),
        grid=(nb, pairs, nq),
        in_specs=in_specs,
        out_specs=pl.BlockSpec((tq, LANES), lambda b, p, i: (b * nq + i, p)),
        out_shape=jax.ShapeDtypeStruct((m, pairs * LANES), BF16),
        scratch_shapes=[pltpu.VMEM((2, seq // tk, LANES, tk), BF16), pltpu.VMEM((2, LANES, tq), F32),
                        pltpu.VMEM((2, tk, tq), F32), pltpu.VMEM((2, tk, tq), BF16),
                        pltpu.VMEM((2, 2 * LANES, tq), BF16)],
        compiler_params=_cparams(("arbitrary", "arbitrary", "arbitrary")),
        name=mode + "_attn",
    )(*args)


def _rope_tables(seq):
    half = ROPE_DIM // 2
    inv_freq = ROPE_THETA ** (-jnp.arange(half, dtype=F32) / half)
    ang = jnp.arange(seq, dtype=F32)[:, None] * inv_freq[None, :]
    cos, sin = jnp.cos(ang), jnp.sin(ang)
    ones = jnp.ones((seq, HEAD_DIM - ROPE_DIM), F32)
    cos_h = jnp.concatenate([cos, cos, ones], axis=1)
    sin_h = jnp.concatenate([-sin, sin, 0.0 * ones], axis=1)
    return jnp.tile(cos_h, (1, 2)), jnp.tile(sin_h, (1, 2))


def _rotary(x, cos_t, sin_t):
    half = ROPE_DIM // 2
    lane = lax.broadcasted_iota(jnp.int32, (x.shape[0], LANES), 1)
    first_half = (lane & (HEAD_DIM - 1)) < half
    outs = []
    for j in range(x.shape[1] // LANES):
        xc = x[:, j * LANES:(j + 1) * LANES]
        up = pltpu.roll(xc, LANES - half, axis=1)
        down = pltpu.roll(xc, half, axis=1)
        outs.append(xc * cos_t + jnp.where(first_half, up, down) * sin_t)
    return jnp.concatenate(outs, axis=1)


def _moba_k_kernel(k_ref, cos_ref, sin_ref, ko_ref, km_ref):
    kr = _rotary(k_ref[...], cos_ref[...], sin_ref[...])
    ko_ref[...] = kr.astype(ko_ref.dtype)
    km_ref[0] = jnp.mean(kr, axis=0, keepdims=True)


def _moba_k_prep(proj, k_col, cos_t, sin_t, seq):
    m = proj.shape[0]
    t = MOBA_BLOCK
    nblk = seq // t
    return pl.pallas_call(
        _moba_k_kernel,
        grid=(m // t,),
        in_specs=[pl.BlockSpec((t, MOBA_WIDTH), lambda i: (i, k_col)),
                  pl.BlockSpec((t, LANES), lambda i: (i % nblk, 0)),
                  pl.BlockSpec((t, LANES), lambda i: (i % nblk, 0))],
        out_specs=[pl.BlockSpec((t, MOBA_WIDTH), lambda i: (i, 0)),
                   pl.BlockSpec((1, 1, MOBA_WIDTH), lambda i: (i, 0, 0))],
        out_shape=[jax.ShapeDtypeStruct((m, MOBA_WIDTH), BF16),
                   jax.ShapeDtypeStruct((m // t, 1, MOBA_WIDTH), F32)],
        compiler_params=_cparams(("parallel",)),
        name="moba_k_prep",
    )(proj, cos_t, sin_t)


def _moba_q_kernel(q_ref, cos_ref, sin_ref, km_ref, qo_ref, ns_ref, *, nblk):
    own = pl.program_id(0) % nblk
    qr = _rotary(q_ref[...], cos_ref[...], sin_ref[...])
    qo_ref[...] = (qr * (HEAD_DIM ** -0.5)).astype(qo_ref.dtype)
    gate_t = _dot_f32_nt(km_ref[0], qr)
    slots = LANES // MOBA_HEADS
    n = lax.broadcasted_iota(jnp.int32, (slots, gate_t.shape[1]), 0)
    flags = []
    for h in range(MOBA_HEADS):
        g = gate_t[h * slots:(h + 1) * slots, :]
        cnt = jnp.zeros(g.shape, jnp.int32)
        for cand in range(slots):
            other = g[cand:cand + 1, :]
            beats = (other > g) | ((other == g) & (n > cand))
            cnt = cnt + jnp.where(beats, jnp.where(cand < own, 1, 0), 0)
        selected = (n == own) | ((n < own) & (cnt < MOBA_TOPK))
        flags.append(jnp.where(selected, 0.0, 1.0))
    ns_ref[...] = jnp.concatenate(flags, axis=0).T.astype(ns_ref.dtype)


def _moba_q_prep(proj, q_col, cos_t, sin_t, km_mat, seq):
    m = proj.shape[0]
    t = MOBA_BLOCK
    nblk = seq // t
    return pl.pallas_call(
        functools.partial(_moba_q_kernel, nblk=nblk),
        grid=(m // t,),
        in_specs=[pl.BlockSpec((t, MOBA_WIDTH), lambda i: (i, q_col)),
                  pl.BlockSpec((t, LANES), lambda i: (i % nblk, 0)),
                  pl.BlockSpec((t, LANES), lambda i: (i % nblk, 0)),
                  pl.BlockSpec((1, LANES, MOBA_WIDTH), lambda i: (i // nblk, 0, 0))],
        out_specs=[pl.BlockSpec((t, MOBA_WIDTH), lambda i: (i, 0)),
                   pl.BlockSpec((t, LANES), lambda i: (i, 0))],
        out_shape=[jax.ShapeDtypeStruct((m, MOBA_WIDTH), BF16),
                   jax.ShapeDtypeStruct((m, LANES), BF16)],
        compiler_params=_cparams(("parallel",)),
        name="moba_q_prep",
    )(proj, cos_t, sin_t, km_mat)


def _pair_cols(x, h0, h1, shape):
    lane = lax.broadcasted_iota(jnp.int32, shape, 1)
    a = jnp.broadcast_to(x[:, h0:h0 + 1], shape)
    b = jnp.broadcast_to(x[:, h1:h1 + 1], shape)
    return jnp.where(lane < SSM_HEAD_DIM, a, b)


def _ssd_kernel(in_ref, cw_ref, cb_ref, dtb_ref, alog_ref, dsk_ref, nw_ref, o_ref, xpad_scr, st_scr):
    t = SSM_CHUNK
    c = pl.program_id(1)
    halo = SUBLANES

    @pl.when(c == 0)
    def _():
        xpad_scr[0:halo, :] = jnp.zeros((halo, SSM_CONV_DIM), F32)
        st_scr[...] = jnp.zeros(st_scr.shape, F32)

    xpad_scr[halo:halo + t, :] = in_ref[:, 0:SSM_CONV_DIM]
    conv = cb_ref[...] + cw_ref[0:1, :] * xpad_scr[pl.ds(halo - SSM_CONV + 1, t), :]
    for k in range(1, SSM_CONV):
        conv = conv + cw_ref[k:k + 1, :] * xpad_scr[pl.ds(halo - SSM_CONV + 1 + k, t), :]
    xpad_scr[0:halo, :] = xpad_scr[t:t + halo, :]
    xbc = conv * _sigmoid(conv)
    z = in_ref[:, SSM_CONV_DIM:SSM_CONV_DIM + SSM_WIDTH]
    dt = _softplus(in_ref[:, SSM_CONV_DIM + SSM_WIDTH:] + dtb_ref[...])
    a = -jnp.exp(alog_ref[...])
    acum = _tri_cumsum(dt * a)
    acum_t = acum.T
    tot = acum[t - 1:t, :]
    decay_end = jnp.exp(tot - acum)
    decay_in = jnp.exp(acum)
    exp_tot = jnp.exp(tot)
    row = lax.broadcasted_iota(jnp.int32, (t, t), 0)
    col = lax.broadcasted_iota(jnp.int32, (t, t), 1)
    causal = row >= col
    lane = lax.broadcasted_iota(jnp.int32, (t, LANES), 1)
    lane1 = lax.broadcasted_iota(jnp.int32, (1, LANES), 1)
    pair_shape = (t, LANES)
    rep = SSM_HEADS // SSM_GROUPS
    b_off = SSM_WIDTH
    c_off = SSM_WIDTH + SSM_GROUPS * SSM_STATE
    ys = []
    for g in range(SSM_GROUPS):
        bm = xbc[:, b_off + g * SSM_STATE:b_off + (g + 1) * SSM_STATE]
        cm = xbc[:, c_off + g * SSM_STATE:c_off + (g + 1) * SSM_STATE].astype(BF16)
        cb = _dot_nt(cm, bm.astype(BF16))
        bm_t = bm.T.astype(BF16)
        for pr in range(rep // 2):
            j = g * (rep // 2) + pr
            h0, h1 = 2 * j, 2 * j + 1
            xp = xbc[:, j * LANES:(j + 1) * LANES]
            xdt = xp * _pair_cols(dt, h0, h1, pair_shape)
            xdt_b = xdt.astype(BF16)
            yd = []
            for h in (h0, h1):
                diff = acum[:, h:h + 1] - acum_t[h:h + 1, :]
                w = cb * jnp.exp(jnp.where(causal, diff, NEG_BIG))
                yd.append(_dot(w.astype(BF16), xdt_b))
            y_diag = jnp.where(lane < SSM_HEAD_DIM, yd[0], yd[1])
            prev = st_scr[j]
            y_off = _dot(cm, prev.astype(BF16)) * _pair_cols(decay_in, h0, h1, pair_shape)
            xw = (xdt * _pair_cols(decay_end, h0, h1, pair_shape)).astype(BF16)
            scale = jnp.where(lane1 < SSM_HEAD_DIM, exp_tot[:, h0:h0 + 1], exp_tot[:, h1:h1 + 1])
            st_scr[j] = prev * scale + _dot(bm_t, xw)
            ys.append(y_diag + y_off + dsk_ref[:, j * LANES:(j + 1) * LANES] * xp)
    y = jnp.concatenate(ys, axis=1)
    y = y * (z * _sigmoid(z))
    gw = SSM_WIDTH // SSM_GROUPS
    outs = []
    for g in range(SSM_GROUPS):
        seg = y[:, g * gw:(g + 1) * gw]
        ms = jnp.mean(seg * seg, axis=-1, keepdims=True)
        outs.append(seg * lax.rsqrt(ms + NORM_EPS) * nw_ref[:, g * gw:(g + 1) * gw])
    o_ref[...] = jnp.concatenate(outs, axis=1).astype(o_ref.dtype)


def _ssd(proj, conv_w, conv_b, dt_bias, a_log, d_skip, norm_w, seq):
    m, wtot = proj.shape
    t = SSM_CHUNK
    nc = seq // t
    pad = LANES - SSM_HEADS
    dtb = jnp.pad(dt_bias, (0, pad)).reshape(1, LANES)
    alog = jnp.pad(a_log, (0, pad)).reshape(1, LANES)
    dsk = jnp.repeat(d_skip, SSM_HEAD_DIM).reshape(1, SSM_WIDTH)
    const = lambda b, c: (0, 0)
    return pl.pallas_call(
        _ssd_kernel,
        grid=(m // seq, nc),
        in_specs=[pl.BlockSpec((t, wtot), lambda b, c: (b * nc + c, 0)),
                  pl.BlockSpec((SSM_CONV, SSM_CONV_DIM), const),
                  pl.BlockSpec((1, SSM_CONV_DIM), const),
                  pl.BlockSpec((1, LANES), const),
                  pl.BlockSpec((1, LANES), const),
                  pl.BlockSpec((1, SSM_WIDTH), const),
                  pl.BlockSpec((1, SSM_WIDTH), const)],
        out_specs=pl.BlockSpec((t, SSM_WIDTH), lambda b, c: (b * nc + c, 0)),
        out_shape=jax.ShapeDtypeStruct((m, SSM_WIDTH), BF16),
        scratch_shapes=[pltpu.VMEM((t + SUBLANES, SSM_CONV_DIM), F32),
                        pltpu.VMEM((SSM_HEADS // 2, SSM_STATE, LANES), F32)],
        compiler_params=_cparams(("parallel", "arbitrary")),
        name="ssd",
    )(proj, conv_w, conv_b.reshape(1, SSM_CONV_DIM), dtb, alog, dsk, norm_w.reshape(1, SSM_WIDTH))


def _merge_kernel(ya_ref, yb_ref, yc_ref, yd_ref, g_ref, x_ref, g1_ref,
                  wa_ref, wb_ref, wc_ref, wd_ref, wo_ref, o_ref):
    d = x_ref.shape[1]
    gate = lambda i: _sigmoid(g_ref[:, i * d:(i + 1) * d].astype(F32))
    merged = gate(0) * _dot(ya_ref[...], wa_ref[...])
    merged = merged + gate(1) * _dot(yb_ref[...], wb_ref[...])
    merged = merged + gate(2) * _dot(yc_ref[...], wc_ref[...])
    merged = merged + gate(3) * _dot(yd_ref[...], wd_ref[...])
    o_ref[...] = x_ref[...] + g1_ref[0] * _dot(merged.astype(BF16), wo_ref[...])


def _merge(ya, yb, yc, yd, gates, x, mod, k_gate, wa, wb, wc, wd, wo, seq, tm):
    m, d = x.shape
    nb = m // seq
    rows = lambda i: (i, 0)
    const = lambda i: (0, 0)
    return pl.pallas_call(
        _merge_kernel,
        grid=(m // tm,),
        in_specs=[pl.BlockSpec((tm, ya.shape[1]), rows),
                  pl.BlockSpec((tm, yb.shape[1]), rows),
                  pl.BlockSpec((tm, yc.shape[1]), rows),
                  pl.BlockSpec((tm, yd.shape[1]), rows),
                  pl.BlockSpec((tm, N_BRANCH * d), rows),
                  pl.BlockSpec((tm, d), rows),
                  pl.BlockSpec((1, 1, d), lambda i: (k_gate * nb + (i * tm) // seq, 0, 0)),
                  pl.BlockSpec(wa.shape, const),
                  pl.BlockSpec(wb.shape, const),
                  pl.BlockSpec(wc.shape, const),
                  pl.BlockSpec(wd.shape, const),
                  pl.BlockSpec(wo.shape, const)],
        out_specs=pl.BlockSpec((tm, d), rows),
        out_shape=jax.ShapeDtypeStruct((m, d), F32),
        compiler_params=_cparams(("parallel",)),
        name="merge",
    )(ya, yb, yc, yd, gates, x, mod, wa, wb, wc, wd, wo)


def _ffn_kernel(x_ref, nw_ref, sc_ref, sh_ref, g_ref, w1_ref, w2_ref, o_ref, h_scr, acc_scr):
    j = pl.program_id(1)

    @pl.when(j == 0)
    def _():
        h_scr[...] = _rms_mod(x_ref[...], nw_ref[...], sc_ref[0], sh_ref[0]).astype(BF16)
        acc_scr[...] = jnp.zeros(acc_scr.shape, F32)

    a = jnp.maximum(_dot(h_scr[...], w1_ref[...]), 0.0)
    acc_scr[...] += _dot((a * a).astype(BF16), w2_ref[...])

    @pl.when(j == pl.num_programs(1) - 1)
    def _():
        o_ref[...] = x_ref[...] + g_ref[0] * acc_scr[...]


def _ffn(x, nw, mod, k_sc, k_sh, k_gate, w1, w2, seq, tm, tf):
    m, d = x.shape
    nb = m // seq
    dff = w1.shape[1]
    modrow = lambda k: (lambda i, j: (k * nb + (i * tm) // seq, 0, 0))
    return pl.pallas_call(
        _ffn_kernel,
        grid=(m // tm, dff // tf),
        in_specs=[pl.BlockSpec((tm, d), lambda i, j: (i, 0)),
                  pl.BlockSpec((1, d), lambda i, j: (0, 0)),
                  pl.BlockSpec((1, 1, d), modrow(k_sc)),
                  pl.BlockSpec((1, 1, d), modrow(k_sh)),
                  pl.BlockSpec((1, 1, d), modrow(k_gate)),
                  pl.BlockSpec((d, tf), lambda i, j: (0, j)),
                  pl.BlockSpec((tf, d), lambda i, j: (j, 0))],
        out_specs=pl.BlockSpec((tm, d), lambda i, j: (i, 0)),
        out_shape=jax.ShapeDtypeStruct((m, d), F32),
        scratch_shapes=[pltpu.VMEM((tm, d), BF16), pltpu.VMEM((tm, d), F32)],
        compiler_params=_cparams(("parallel", "arbitrary"), FFN_VMEM_LIMIT),
        name="ffn",
    )(x, nw.reshape(1, d), mod, mod, mod, w1, w2)


def _pad_rows(w, rows):
    return jnp.pad(w, ((0, rows - w.shape[0]), (0, 0)))


def _split_w_in(w_in, l):
    wt = jnp.transpose(w_in, (2, 0, 1))[:, l, :]
    o = 0
    uv = wt[o:o + 2 * GMLP_WIDTH]; o += 2 * GMLP_WIDTH
    fox_qk = wt[o:o + 2 * FOX_WIDTH]; o += 2 * FOX_WIDTH
    fox_v = wt[o:o + FOX_WIDTH]; o += FOX_WIDTH
    fox_f = wt[o:o + FOX_HEADS]; o += FOX_HEADS
    moba_qk = wt[o:o + 2 * MOBA_WIDTH]; o += 2 * MOBA_WIDTH
    moba_v = wt[o:o + MOBA_WIDTH]; o += MOBA_WIDTH
    z = wt[o:o + SSM_WIDTH]; o += SSM_WIDTH
    xbc = wt[o:o + SSM_CONV_DIM]; o += SSM_CONV_DIM
    dt = wt[o:o + SSM_HEADS]; o += SSM_HEADS
    gates = wt[o:]
    w_v = jnp.concatenate([fox_v, moba_v], axis=0)
    w_f32 = jnp.concatenate([uv, moba_qk, _pad_rows(fox_f, LANES)], axis=0)
    w_ssd = jnp.concatenate([xbc, z, _pad_rows(dt, LANES)], axis=0)
    return [g.astype(BF16) for g in (fox_qk, w_v, w_f32, w_ssd, gates)]


def _tile_rows(seq, want):
    return min(want, seq)


def kernel(x, c, ada_w, ada_b, norm_mix_w, w_in, gmlp_ln_w, gmlp_ln_b, gmlp_ws, gmlp_bs, fox_f_bias, ssm_conv_w, ssm_conv_b, ssm_dt_bias, ssm_a_log, ssm_d, ssm_norm_w, w_branch_a, w_branch_b, w_branch_c, w_branch_d, w_out, norm_mlp_w, mlp_w1, mlp_w2, final_norm_w):
    nb, seq, d = x.shape
    m = nb * seq
    depth = ada_w.shape[0]
    tm = _tile_rows(seq, 512)
    tm_proj = _tile_rows(seq, 1024)
    tq_attn = _tile_rows(seq, ATTN_Q_TILE)
    pairs = FOX_HEADS // 2
    nblk = seq // MOBA_BLOCK
    slots = LANES // MOBA_HEADS
    assert seq % MOBA_BLOCK == 0 and nblk <= slots and seq % SSM_CHUNK == 0

    mod_all = _modulation(c, ada_w, ada_b)[:, :nb]
    mod_all = mod_all.reshape(depth, nb, 6, d).transpose(0, 2, 1, 3).reshape(depth, 6 * nb, 1, d)
    cos_t, sin_t = _rope_tables(seq)
    head_eye = jnp.eye(MOBA_HEADS, dtype=F32)

    xf = x.reshape(m, d)
    for l in range(depth):
        mod = mod_all[l]
        w_fqk, w_v, w_f32, w_ssd, w_gate = _split_w_in(w_in, l)
        h = _norm_mod(xf, norm_mix_w[l], mod, 1, 0, seq, tm)
        p_fqk = _matmul(h, w_fqk, BF16, tm_proj, w_fqk.shape[0], "proj_fox_qk")
        p_vt = _matmul_t(h, w_v, BF16, tm_proj, "proj_vt")
        p_f32 = _matmul(h, w_f32, F32, tm_proj, w_f32.shape[0], "proj_f32")
        mq_col = 2 * GMLP_WIDTH // MOBA_WIDTH
        f_col = (2 * GMLP_WIDTH + 2 * MOBA_WIDTH) // LANES
        p_ssd = _matmul(h, w_ssd, F32, tm_proj, w_ssd.shape[0], "proj_ssd")
        p_gate = _matmul(h, w_gate, BF16, tm_proj, w_gate.shape[0], "proj_gate")

        y_a = _gmlp(p_f32, gmlp_ln_w[l], gmlp_ln_b[l], gmlp_ws[l], gmlp_bs[l], tm)

        k_aux = _forget_cumsum(p_f32, f_col, fox_f_bias[l], seq)
        y_b = _block_attention("fox", p_fqk, 0, k_aux, p_fqk, pairs, p_vt, 0, seq, tq_attn, MOBA_BLOCK)

        k_rot, k_mean = _moba_k_prep(p_f32, mq_col + 1, cos_t, sin_t, seq)
        km = jnp.pad(k_mean.reshape(nb, nblk, MOBA_HEADS, HEAD_DIM), ((0, 0), (0, slots - nblk), (0, 0), (0, 0)))
        km_mat = jnp.einsum('bnhd,kh->bknhd', km, head_eye).reshape(nb, LANES, MOBA_WIDTH)
        q_rot, not_sel = _moba_q_prep(p_f32, mq_col, cos_t, sin_t, km_mat, seq)
        y_c = _block_attention("moba", q_rot, 0, not_sel, k_rot, 0, p_vt, pairs, seq, tq_attn, MOBA_BLOCK)

        y_d = _ssd(p_ssd, ssm_conv_w[l], ssm_conv_b[l], ssm_dt_bias[l], ssm_a_log[l], ssm_d[l],
                   ssm_norm_w[l], seq)

        xf = _merge(y_a, y_b, y_c, y_d, p_gate, xf, mod, 2,
                    w_branch_a[l].astype(BF16), w_branch_b[l].astype(BF16), w_branch_c[l].astype(BF16),
                    w_branch_d[l].astype(BF16), w_out[l].astype(BF16), seq, tm)
        xf = _ffn(xf, norm_mlp_w[l], mod, 4, 3, 5, mlp_w1[l].astype(BF16), mlp_w2[l].astype(BF16),
                  seq, tm, mlp_w1.shape[2])
    return _final_norm(xf, final_norm_w, tm).reshape(nb, seq, d)
```
